```python
import jax, jax.numpy as jnp
from jax import lax
import numpy as np

D_MODEL = 2048
BATCH = 2
SEQ = 8192
DEPTH = 4

MEM_LEN = 256
NORM_EPS = 1e-6

SWA_Q_HEADS = 16
SWA_KV_HEADS = 4
SWA_HEAD_DIM = 64
SWA_WINDOW = 128
SWA_BLOCK = 128
SWA_Q_W = SWA_Q_HEADS * SWA_HEAD_DIM
SWA_KV_W = SWA_KV_HEADS * SWA_HEAD_DIM

RET_HEADS = 8
RET_HEAD_DIM = 128
RET_CHUNK = 128
RET_W = RET_HEADS * RET_HEAD_DIM
RET_ROPE_BASE = 10000.0

LRU_WIDTH = 1024
LRU_BLOCKS = 8
LRU_BLOCK_DIM = LRU_WIDTH // LRU_BLOCKS
LRU_CONV = 4
LRU_C = 8.0

N_BRANCH = 3
BRANCH_W = 1024
IN_SPLIT_SIZES = (SWA_Q_W, SWA_KV_W, SWA_KV_W,
                  RET_W, RET_W, RET_W, RET_W,
                  LRU_WIDTH, LRU_WIDTH,
                  D_MODEL, D_MODEL, D_MODEL)
IN_W = sum(IN_SPLIT_SIZES)

XATTN_HEADS = 4
XATTN_HEAD_DIM = D_MODEL // XATTN_HEADS

FFN_DIM = 5504
FFN_CONV = 3

kernel_name = "hybrid_swa_retention_rglru_convffn_trunk"


def rms_norm(x, g):
    x32 = x.astype(jnp.float32)
    y = x32 * lax.rsqrt(jnp.mean(x32 * x32, axis=-1, keepdims=True) + NORM_EPS)
    return (y * g.astype(jnp.float32)).astype(x.dtype)


def causal_dwconv(x, w, b):
    width = w.shape[0]
    y = lax.conv_general_dilated(
        x, w[:, None, :].astype(x.dtype), window_strides=(1,), padding=[(width - 1, 0)],
        dimension_numbers=("NWC", "WIO", "NWC"), feature_group_count=x.shape[-1])
    return y + b.astype(x.dtype)


def sliding_window_attention(q, k, v, sinks):
    b, s, hq, dh = q.shape
    hkv = k.shape[2]
    g = hq // hkv
    nb = s // SWA_BLOCK
    qb = q.reshape(b, nb, SWA_BLOCK, hkv, g, dh)
    kb = k.reshape(b, nb, SWA_BLOCK, hkv, dh)
    vb = v.reshape(b, nb, SWA_BLOCK, hkv, dh)

    def with_prev(t):
        prev = jnp.pad(t, ((0, 0), (1, 0), (0, 0), (0, 0), (0, 0)))[:, :-1]
        return jnp.concatenate([prev, t], axis=2)

    kw = with_prev(kb)
    vw = with_prev(vb)
    scores = jnp.einsum("bnqhgd,bnkhd->bnhgqk", qb, kw).astype(jnp.float32) * (dh ** -0.5)
    blk = jnp.arange(nb)[:, None] * SWA_BLOCK
    q_pos = blk + jnp.arange(SWA_BLOCK)[None, :]
    k_pos = blk - SWA_BLOCK + jnp.arange(2 * SWA_BLOCK)[None, :]
    rel = q_pos[:, :, None] - k_pos[:, None, :]
    allowed = (rel >= 0) & (rel < SWA_WINDOW) & (k_pos[:, None, :] >= 0)
    scores = jnp.where(allowed[None, :, None, None], scores, -jnp.inf)
    sink = sinks.astype(jnp.float32).reshape(hkv, g)[None, None, :, :, None, None]
    m = jnp.maximum(jnp.max(scores, axis=-1, keepdims=True), sink)
    p = jnp.exp(scores - m)
    probs = p / (jnp.sum(p, axis=-1, keepdims=True) + jnp.exp(sink - m))
    out = jnp.einsum("bnhgqk,bnkhd->bnqhgd", probs.astype(v.dtype), vw)
    return out.reshape(b, s, hq * dh)


def rotate(t, cos, sin):
    t1, t2 = jnp.split(t, 2, axis=-1)
    return jnp.concatenate([t1 * cos - t2 * sin, t1 * sin + t2 * cos], axis=-1)


def retention(q, k, v):
    b, s, h, d = q.shape
    nc = s // RET_CHUNK
    log_gamma = jnp.log1p(-jnp.exp2(-5.0 - jnp.arange(h, dtype=jnp.float32)))
    qc = q.reshape(b, nc, RET_CHUNK, h, d)
    kc = k.reshape(b, nc, RET_CHUNK, h, d)
    vc = v.reshape(b, nc, RET_CHUNK, h, d)
    idx = jnp.arange(RET_CHUNK, dtype=jnp.float32)
    diff = idx[:, None] - idx[None, :]
    intra_decay = jnp.where(diff[None] >= 0,
                            jnp.exp(jnp.maximum(diff, 0.0)[None] * log_gamma[:, None, None]), 0.0)
    scores = jnp.einsum("bnqhd,bnkhd->bnhqk", qc, kc) * intra_decay[None, None]
    intra = jnp.einsum("bnhqk,bnkhd->bnqhd", scores, vc)
    k_decay = jnp.exp((RET_CHUNK - 1.0 - idx)[:, None] * log_gamma[None, :])
    q_decay = jnp.exp((idx + 1.0)[:, None] * log_gamma[None, :])
    chunk_kv = jnp.einsum("bnkhd,bnkhe->nbhde", kc * k_decay[:, :, None], vc)
    chunk_decay = jnp.exp(RET_CHUNK * log_gamma)[None, :, None, None]

    def step(state, kv):
        return state * chunk_decay + kv, state

    _, prev_states = lax.scan(step, jnp.zeros((b, h, d, d), jnp.float32), chunk_kv)
    inter = jnp.einsum("bnqhd,nbhde->bnqhe", qc * q_decay[:, :, None], prev_states)
    return (intra + inter).reshape(b, s, h, d)


def head_group_norm(y):
    mu = jnp.mean(y, axis=-1, keepdims=True)
    yc = y - mu
    return yc * lax.rsqrt(jnp.mean(yc * yc, axis=-1, keepdims=True) + NORM_EPS)


def rg_lru(x, w_a, b_a, w_x, b_x, lam):
    b, s, w = x.shape
    xb = x.reshape(b, s, LRU_BLOCKS, LRU_BLOCK_DIM)
    gate_r = jax.nn.sigmoid(jnp.einsum("bsnd,nde->bsne", xb, w_a).reshape(b, s, w) + b_a)
    gate_i = jax.nn.sigmoid(jnp.einsum("bsnd,nde->bsne", xb, w_x).reshape(b, s, w) + b_x)
    log_a = -LRU_C * gate_r.astype(jnp.float32) * jax.nn.softplus(-lam.astype(jnp.float32))
    a = jnp.exp(log_a)
    u = jnp.sqrt(-jnp.expm1(2.0 * log_a)) * (gate_i * x).astype(jnp.float32)

    def combine(left, right):
        a_l, b_l = left
        a_r, b_r = right
        return a_l * a_r, a_r * b_l + b_r

    _, hs = lax.associative_scan(combine, (a, u), axis=1)
    return hs.astype(x.dtype)


def hybrid_mixer(h, w_in, attn_sinks, lru_conv_w, lru_conv_b, lru_wa, lru_ba, lru_wx, lru_bx,
                 lru_lambda, w_branch, w_out):
    b, s, _ = h.shape
    proj = h @ w_in
    parts = []
    off = 0
    for size in IN_SPLIT_SIZES:
        parts.append(proj[..., off:off + size])
        off += size
    q_a, k_a, v_a, q_r, k_r, v_r, g_r, x_c, y_c, gate_a, gate_b, gate_c = parts

    o_a = sliding_window_attention(q_a.reshape(b, s, SWA_Q_HEADS, SWA_HEAD_DIM),
                                   k_a.reshape(b, s, SWA_KV_HEADS, SWA_HEAD_DIM),
                                   v_a.reshape(b, s, SWA_KV_HEADS, SWA_HEAD_DIM), attn_sinks)

    theta = RET_ROPE_BASE ** (-jnp.linspace(0.0, 1.0, RET_HEAD_DIM // 2, dtype=jnp.float32))
    ang = jnp.arange(s, dtype=jnp.float32)[:, None] * theta[None, :]
    cos = jnp.cos(ang)[:, None, :]
    sin = jnp.sin(ang)[:, None, :]
    qr = rotate(q_r.astype(jnp.float32).reshape(b, s, RET_HEADS, RET_HEAD_DIM), cos, sin)
    kr = rotate(k_r.astype(jnp.float32).reshape(b, s, RET_HEADS, RET_HEAD_DIM), cos, sin) * (RET_HEAD_DIM ** -0.5)
    vr = v_r.astype(jnp.float32).reshape(b, s, RET_HEADS, RET_HEAD_DIM)
    y_ret = head_group_norm(retention(qr, kr, vr)).reshape(b, s, RET_W).astype(h.dtype)
    o_b = jax.nn.silu(g_r) * y_ret

    xc = causal_dwconv(x_c, lru_conv_w, lru_conv_b)
    o_c = rg_lru(xc, lru_wa, lru_ba, lru_wx, lru_bx, lru_lambda) * jax.nn.gelu(y_c)

    merged = (jax.nn.sigmoid(gate_a) * (o_a @ w_branch[0])
              + jax.nn.sigmoid(gate_b) * (o_b @ w_branch[1])
              + jax.nn.sigmoid(gate_c) * (o_c @ w_branch[2]))
    return merged @ w_out


def memory_cross_attention(h, mem_n, wq, wkv, wo):
    b, s, _ = h.shape
    m = mem_n.shape[1]
    q = (h @ wq).reshape(b, s, XATTN_HEADS, XATTN_HEAD_DIM)
    kv = mem_n @ wkv
    k = kv[..., :D_MODEL].reshape(b, m, XATTN_HEADS, XATTN_HEAD_DIM)
    v = kv[..., D_MODEL:].reshape(b, m, XATTN_HEADS, XATTN_HEAD_DIM)
    scores = jnp.einsum("bshd,bmhd->bhsm", q, k).astype(jnp.float32) * (XATTN_HEAD_DIM ** -0.5)
    p = jax.nn.softmax(scores, axis=-1).astype(v.dtype)
    o = jnp.einsum("bhsm,bmhd->bshd", p, v).reshape(b, s, D_MODEL)
    return o @ wo


def conv_ffn(h, w_up, conv_w, conv_b, w_down):
    gate = causal_dwconv(h @ w_up[:, :FFN_DIM], conv_w, conv_b)
    val = h @ w_up[:, FFN_DIM:]
    return (jax.nn.silu(gate) * val) @ w_down


def setup_inputs(seed: int = 0) -> dict:
    key = jax.random.key(seed)
    ks = jax.random.split(key, 28)
    f32 = jnp.float32
    L = DEPTH

    def dense(k, shape, fan_in):
        return jax.random.normal(k, shape, f32) * (fan_in ** -0.5)

    def gain(k, shape):
        return 1.0 + 0.02 * jax.random.normal(k, shape, f32)

    def small(k, shape, scale=0.02):
        return scale * jax.random.normal(k, shape, f32)

    a_pow = jax.random.uniform(ks[10], (L, LRU_WIDTH), f32, 0.9, 0.999)
    a_base = a_pow ** (1.0 / LRU_C)
    lru_lambda = jnp.log(a_base) - jnp.log1p(-a_base)
    return {
        "x": jax.random.normal(ks[0], (BATCH, SEQ, D_MODEL), f32),
        "mem": jax.random.normal(ks[1], (BATCH, MEM_LEN, D_MODEL), f32),
        "mem_norm_g": gain(ks[2], (D_MODEL,)),
        "norm_mix_g": gain(ks[3], (L, D_MODEL)),
        "w_in": dense(ks[4], (L, D_MODEL, IN_W), D_MODEL),
        "attn_sinks": small(ks[5], (L, SWA_Q_HEADS), 0.5),
        "lru_conv_w": dense(ks[6], (L, LRU_CONV, LRU_WIDTH), LRU_CONV),
        "lru_conv_b": small(ks[7], (L, LRU_WIDTH)),
        "lru_wa": dense(ks[8], (L, LRU_BLOCKS, LRU_BLOCK_DIM, LRU_BLOCK_DIM), LRU_BLOCK_DIM),
        "lru_ba": small(ks[9], (L, LRU_WIDTH), 0.1),
        "lru_wx": dense(ks[11], (L, LRU_BLOCKS, LRU_BLOCK_DIM, LRU_BLOCK_DIM), LRU_BLOCK_DIM),
        "lru_bx": small(ks[12], (L, LRU_WIDTH), 0.1),
        "lru_lambda": lru_lambda,
        "w_branch": dense(ks[13], (L, N_BRANCH, BRANCH_W, D_MODEL), BRANCH_W),
        "w_out": dense(ks[14], (L, D_MODEL, D_MODEL), D_MODEL),
        "norm_xattn_g": gain(ks[15], (L, D_MODEL)),
        "xattn_wq": dense(ks[16], (L, D_MODEL, D_MODEL), D_MODEL),
        "xattn_wkv": dense(ks[17], (L, D_MODEL, 2 * D_MODEL), D_MODEL),
        "xattn_wo": dense(ks[18], (L, D_MODEL, D_MODEL), D_MODEL),
        "norm_ffn_g": gain(ks[19], (L, D_MODEL)),
        "ffn_w_up": dense(ks[20], (L, D_MODEL, 2 * FFN_DIM), D_MODEL),
        "ffn_conv_w": dense(ks[21], (L, FFN_CONV, FFN_DIM), FFN_CONV),
        "ffn_conv_b": small(ks[22], (L, FFN_DIM)),
        "ffn_w_down": dense(ks[23], (L, FFN_DIM, D_MODEL), FFN_DIM),
        "final_norm_g": gain(ks[24], (D_MODEL,)),
    }


def reference(x, mem, mem_norm_g, norm_mix_g, w_in, attn_sinks, lru_conv_w, lru_conv_b, lru_wa,
              lru_ba, lru_wx, lru_bx, lru_lambda, w_branch, w_out, norm_xattn_g, xattn_wq,
              xattn_wkv, xattn_wo, norm_ffn_g, ffn_w_up, ffn_conv_w, ffn_conv_b, ffn_w_down,
              final_norm_g):
    mem_n = rms_norm(mem, mem_norm_g)
    for l in range(DEPTH):
        x = x + hybrid_mixer(rms_norm(x, norm_mix_g[l]), w_in[l], attn_sinks[l], lru_conv_w[l],
                             lru_conv_b[l], lru_wa[l], lru_ba[l], lru_wx[l], lru_bx[l],
                             lru_lambda[l], w_branch[l], w_out[l])
        x = x + memory_cross_attention(rms_norm(x, norm_xattn_g[l]), mem_n, xattn_wq[l],
                                       xattn_wkv[l], xattn_wo[l])
        x = x + conv_ffn(rms_norm(x, norm_ffn_g[l]), ffn_w_up[l], ffn_conv_w[l], ffn_conv_b[l],
                         ffn_w_down[l])
    return rms_norm(x, final_norm_g)
```

```python
import functools

import numpy as np
import jax
import jax.numpy as jnp
from jax import lax
from jax.experimental import pallas as pl
from jax.experimental.pallas import tpu as pltpu

NORM_EPS = 1e-6

LANES = 128
SUBLANES = 8
VMEM_BUDGET_BYTES = 56 * 1024 * 1024

SWA_Q_HEADS = 16
SWA_KV_HEADS = 4
SWA_HEAD_DIM = 64
SWA_BLOCK = 128
SWA_Q_W = SWA_Q_HEADS * SWA_HEAD_DIM
SWA_KV_W = SWA_KV_HEADS * SWA_HEAD_DIM

RET_HEADS = 8
RET_HEAD_DIM = 128
RET_CHUNK = 128
RET_W = RET_HEADS * RET_HEAD_DIM
RET_ROPE_BASE = 10000.0

LRU_WIDTH = 1024
LRU_BLOCKS = 8
LRU_BLOCK_DIM = LRU_WIDTH // LRU_BLOCKS
LRU_CONV = 4
LRU_C = 8.0

XATTN_HEADS = 4

FFN_CONV = 3
FFN_TILE_N = 512

BF16 = jnp.bfloat16
F32 = jnp.float32


def _compiler_params(semantics, vmem_bytes):
    return pltpu.CompilerParams(
        dimension_semantics=semantics,
        vmem_limit_bytes=int(min(max(vmem_bytes, 16 * 1024 * 1024), VMEM_BUDGET_BYTES)))


def _nbytes(shape, dtype):
    return int(np.prod(shape)) * jnp.dtype(dtype).itemsize


def _dot(a, b):
    return jnp.dot(a, b, preferred_element_type=F32)


def _dot_nt(a, b):
    return lax.dot_general(a, b, (((1,), (1,)), ((), ())), preferred_element_type=F32)


def _rms(x, g):
    ms = jnp.mean(x * x, axis=-1, keepdims=True)
    return x * lax.rsqrt(ms + NORM_EPS) * g


def _rmsnorm_kernel(x_ref, g_ref, o_ref):
    o_ref[...] = _rms(x_ref[...], g_ref[...]).astype(o_ref.dtype)


def rmsnorm(x, g, out_dtype=BF16):
    t, d = x.shape
    tm = min(512, t)
    blocks = 2 * (_nbytes((tm, d), F32) + _nbytes((tm, d), out_dtype)) + 2 * _nbytes((tm, d), F32)
    return pl.pallas_call(
        _rmsnorm_kernel,
        grid=(t // tm,),
        in_specs=[pl.BlockSpec((tm, d), lambda i: (i, 0)),
                  pl.BlockSpec((1, d), lambda i: (0, 0))],
        out_specs=pl.BlockSpec((tm, d), lambda i: (i, 0)),
        out_shape=jax.ShapeDtypeStruct((t, d), out_dtype),
        compiler_params=_compiler_params(("parallel",), blocks),
        name="rmsnorm",
    )(x, g.reshape(1, d))


def _matmul_kernel(a_ref, w_ref, o_ref):
    o_ref[...] = _dot(a_ref[...], w_ref[...]).astype(o_ref.dtype)


def matmul(a, w, tm, tn, name):
    m, k = a.shape
    n = w.shape[1]
    tm = min(tm, m)
    blocks = (2 * (_nbytes((tm, k), BF16) + _nbytes((k, tn), BF16) + _nbytes((tm, tn), BF16))
              + 2 * _nbytes((tm, tn), F32))
    return pl.pallas_call(
        _matmul_kernel,
        grid=(m // tm, n // tn),
        in_specs=[pl.BlockSpec((tm, k), lambda i, j: (i, 0)),
                  pl.BlockSpec((k, tn), lambda i, j: (0, j))],
        out_specs=pl.BlockSpec((tm, tn), lambda i, j: (i, j)),
        out_shape=jax.ShapeDtypeStruct((m, n), BF16),
        compiler_params=_compiler_params(("parallel", "arbitrary"), blocks),
        name=name,
    )(a, w)


def _q_head_order():
    order = []
    for kv_pair in range(SWA_KV_HEADS // 2):
        for i in range(SWA_Q_HEADS // SWA_KV_HEADS):
            g = SWA_Q_HEADS // SWA_KV_HEADS
            order += [(2 * kv_pair) * g + i, (2 * kv_pair + 1) * g + i]
    return order


def _swa_kernel(sinks_ref, q_ref, kc_ref, vc_ref, kp_ref, vp_ref, o_ref, *, tq, tiles_per_seq):
    i = pl.program_id(0)
    starts_seq = (i % tiles_per_seq) == 0
    blk = SWA_BLOCK
    kwin = jnp.concatenate([kp_ref[...], kc_ref[...]], axis=0)
    vwin = jnp.concatenate([vp_ref[...], vc_ref[...]], axis=0)
    lane = lax.broadcasted_iota(jnp.int32, (blk, LANES), 1)
    lo = lane < SWA_HEAD_DIM
    row = lax.broadcasted_iota(jnp.int32, (2 * blk, 2 * blk), 0)
    col = lax.broadcasted_iota(jnp.int32, (2 * blk, 2 * blk), 1)
    qi = jnp.bitwise_and(row, blk - 1)
    in_window = jnp.logical_and(col > qi, col <= qi + blk)
    in_window_first = jnp.logical_and(in_window, jnp.logical_or(col >= blk, jnp.logical_not(starts_seq)))
    top = lax.broadcasted_iota(jnp.int32, (2 * blk, 1), 0) < blk
    scale = SWA_HEAD_DIM ** -0.5
    order = _q_head_order()
    for c in range(tq // blk):
        allowed = in_window_first if c == 0 else in_window
        for p in range(SWA_Q_W // LANES):
            kv_blk = p // (SWA_Q_W // LANES // (SWA_KV_W // LANES))
            kb = kwin[c * blk:(c + 2) * blk, kv_blk * LANES:(kv_blk + 1) * LANES]
            vb = vwin[c * blk:(c + 2) * blk, kv_blk * LANES:(kv_blk + 1) * LANES]
            qb = q_ref[c * blk:(c + 1) * blk, p * LANES:(p + 1) * LANES]
            zero = jnp.zeros_like(qb)
            qs = jnp.concatenate([jnp.where(lo, qb, zero), jnp.where(lo, zero, qb)], axis=0)
            s = _dot_nt(qs, kb) * scale
            s = jnp.where(allowed, s, -jnp.inf)
            sink = jnp.where(top, sinks_ref[order[2 * p]], sinks_ref[order[2 * p + 1]])
            m = jnp.maximum(jnp.max(s, axis=-1, keepdims=True), sink)
            e = jnp.exp(s - m)
            denom = jnp.sum(e, axis=-1, keepdims=True) + jnp.exp(sink - m)
            probs = (e * (1.0 / denom)).astype(BF16)
            o = _dot(probs, vb)
            ob = jnp.where(lo, o[:blk], o[blk:])
            o_ref[c * blk:(c + 1) * blk, p * LANES:(p + 1) * LANES] = ob.astype(o_ref.dtype)


def swa_attention(proj, sinks, seq, col_q, col_k, col_v):
    t = proj.shape[0]
    tq = min(512, seq)
    nsub = tq // SWA_BLOCK
    qb, kb, vb = col_q // SWA_Q_W, col_k // SWA_KV_W, col_v // SWA_KV_W
    blocks = 2 * (2 * _nbytes((tq, SWA_Q_W), BF16) + 2 * _nbytes((tq + SWA_BLOCK, SWA_KV_W), BF16))
    blocks += 16 * _nbytes((2 * SWA_BLOCK, 2 * SWA_BLOCK), F32)
    kernel = functools.partial(_swa_kernel, tq=tq, tiles_per_seq=seq // tq)
    return pl.pallas_call(
        kernel,
        grid=(t // tq,),
        in_specs=[
            pl.BlockSpec(memory_space=pltpu.SMEM),
            pl.BlockSpec((tq, SWA_Q_W), lambda i: (i, qb)),
            pl.BlockSpec((tq, SWA_KV_W), lambda i: (i, kb)),
            pl.BlockSpec((tq, SWA_KV_W), lambda i: (i, vb)),
            pl.BlockSpec((SWA_BLOCK, SWA_KV_W), lambda i: (jnp.maximum(i * nsub - 1, 0), kb)),
            pl.BlockSpec((SWA_BLOCK, SWA_KV_W), lambda i: (jnp.maximum(i * nsub - 1, 0), vb)),
        ],
        out_specs=pl.BlockSpec((tq, SWA_Q_W), lambda i: (i, 0)),
        out_shape=jax.ShapeDtypeStruct((t, SWA_Q_W), BF16),
        compiler_params=_compiler_params(("parallel",), blocks),
        name="swa_attention",
    )(sinks, proj, proj, proj, proj, proj)


_RET_LOG_GAMMA = [float(v) for v in
                  np.log1p(-np.exp2(-5.0 - np.arange(RET_HEADS, dtype=np.float32))).astype(np.float32)]


def _retention_kernel(q_ref, k_ref, v_ref, g_ref, cos_ref, sin_ref, o_ref, state_ref, decay_ref):
    c = RET_CHUNK
    d = RET_HEAD_DIM

    @pl.when(pl.program_id(1) == 0)
    def _():
        state_ref[...] = jnp.zeros_like(state_ref)
        qpos = lax.broadcasted_iota(jnp.int32, (c, c), 0)
        kpos = lax.broadcasted_iota(jnp.int32, (c, c), 1)
        diff = (qpos - kpos).astype(F32)
        for h in range(RET_HEADS):
            decay_ref[h] = jnp.where(diff >= 0, jnp.exp(jnp.maximum(diff, 0.0) * _RET_LOG_GAMMA[h]), 0.0)

    cosf = cos_ref[...]
    sinf = sin_ref[...]
    idx = lax.broadcasted_iota(jnp.int32, (c, 1), 0).astype(F32)
    for h in range(RET_HEADS):
        lg = _RET_LOG_GAMMA[h]
        sl = slice(h * d, (h + 1) * d)
        q = q_ref[:, sl].astype(F32)
        k = k_ref[:, sl].astype(F32)
        v = v_ref[:, sl]
        qr = q * cosf + pltpu.roll(q, d // 2, axis=1) * sinf
        kr = (k * cosf + pltpu.roll(k, d // 2, axis=1) * sinf) * (d ** -0.5)
        scores = _dot_nt(qr.astype(BF16), kr.astype(BF16)) * decay_ref[h]
        intra = _dot(scores.astype(BF16), v)
        q_decay = jnp.exp((idx + 1.0) * lg)
        k_decay = jnp.exp((c - 1.0 - idx) * lg)
        state = state_ref[h]
        inter = _dot((qr * q_decay).astype(BF16), state.astype(BF16))
        kd_t = (kr * k_decay).T.astype(BF16)
        state_ref[h] = state * float(np.exp(np.float32(c) * np.float32(lg))) + _dot(kd_t, v)
        y = intra + inter
        mu = jnp.mean(y, axis=-1, keepdims=True)
        yc = y - mu
        yn = yc * lax.rsqrt(jnp.mean(yc * yc, axis=-1, keepdims=True) + NORM_EPS)
        gate = g_ref[:, sl].astype(F32)
        o_ref[:, sl] = (jax.nn.silu(gate) * yn).astype(o_ref.dtype)


def _rope_tables(seq):
    half = RET_HEAD_DIM // 2
    theta = RET_ROPE_BASE ** (-jnp.linspace(0.0, 1.0, half, dtype=F32))
    ang = jnp.arange(seq, dtype=F32)[:, None] * theta[None, :]
    cos = jnp.cos(ang)
    sin = jnp.sin(ang)
    return jnp.concatenate([cos, cos], axis=-1), jnp.concatenate([-sin, sin], axis=-1)


def retention(proj, cosf, sinf, batch, seq, col_q):
    t = proj.shape[0]
    c = RET_CHUNK
    nc = seq // c
    b0 = col_q // RET_W
    blocks = 2 * 5 * _nbytes((c, RET_W), BF16) + 4 * _nbytes((c, RET_HEAD_DIM), F32)
    blocks += 2 * _nbytes((RET_HEADS, RET_HEAD_DIM, RET_HEAD_DIM), F32) + 64 * _nbytes((c, c), F32)

    def col(j):
        return pl.BlockSpec((c, RET_W), lambda b, n: (b * nc + n, b0 + j))

    return pl.pallas_call(
        _retention_kernel,
        grid=(batch, nc),
        in_specs=[col(0), col(1), col(2), col(3),
                  pl.BlockSpec((c, RET_HEAD_DIM), lambda b, n: (n, 0)),
                  pl.BlockSpec((c, RET_HEAD_DIM), lambda b, n: (n, 0))],
        out_specs=pl.BlockSpec((c, RET_W), lambda b, n: (b * nc + n, 0)),
        out_shape=jax.ShapeDtypeStruct((t, RET_W), BF16),
        scratch_shapes=[pltpu.VMEM((RET_HEADS, RET_HEAD_DIM, RET_HEAD_DIM), F32),
                        pltpu.VMEM((RET_HEADS, c, c), F32)],
        compiler_params=_compiler_params(("arbitrary", "arbitrary"), blocks),
        name="retention",
    )(proj, proj, proj, proj, cosf, sinf)


def _lru_kernel(x_ref, y_ref, cw_ref, cb_ref, wa_ref, ba_ref, wx_ref, bx_ref, lam_ref, o_ref,
                xprev_ref, carry_ref, *, tt):
    w = LRU_WIDTH

    @pl.when(pl.program_id(1) == 0)
    def _():
        xprev_ref[...] = jnp.zeros_like(xprev_ref)
        carry_ref[...] = jnp.zeros_like(carry_ref)

    x = x_ref[...].astype(F32)
    xe = jnp.concatenate([xprev_ref[...], x], axis=0)
    xprev_ref[...] = x[tt - SUBLANES:, :]
    cw = cw_ref[...]
    xc = cb_ref[...] + cw[3:4, :] * x
    for kk in range(LRU_CONV - 1):
        off = SUBLANES - (LRU_CONV - 1) + kk
        xc = xc + cw[kk:kk + 1, :] * xe[off:off + tt, :]

    xcb = xc.astype(BF16)
    gr = []
    gi = []
    for blk in range(LRU_BLOCKS):
        sl = slice(blk * LRU_BLOCK_DIM, (blk + 1) * LRU_BLOCK_DIM)
        gr.append(_dot(xcb[:, sl], wa_ref[blk]))
        gi.append(_dot(xcb[:, sl], wx_ref[blk]))
    gate_r = jax.nn.sigmoid(jnp.concatenate(gr, axis=1) + ba_ref[...])
    gate_i = jax.nn.sigmoid(jnp.concatenate(gi, axis=1) + bx_ref[...])
    z = -lam_ref[...]
    softplus = jnp.maximum(z, 0.0) + jnp.log1p(jnp.exp(-jnp.abs(z)))
    log_a = (-LRU_C) * gate_r * softplus
    a = jnp.exp(log_a)
    u = jnp.sqrt(1.0 - a * a) * (gate_i * xc)

    nv = tt // SUBLANES
    a3 = a.reshape(nv, SUBLANES, w)
    b3 = u.reshape(nv, SUBLANES, w)
    rix = lax.broadcasted_iota(jnp.int32, (nv, SUBLANES, w), 1)
    s = 1
    while s < SUBLANES:
        keep = rix >= s
        a_sh = jnp.where(keep, pltpu.roll(a3, s, axis=1), 1.0)
        b_sh = jnp.where(keep, pltpu.roll(b3, s, axis=1), 0.0)
        b3 = a3 * b_sh + b3
        a3 = a3 * a_sh
        s *= 2
    carry = carry_ref[...]
    hs = []
    for i in range(nv):
        h_i = a3[i] * carry + b3[i]
        carry = h_i[SUBLANES - 1:SUBLANES, :]
        hs.append(h_i)
    carry_ref[...] = carry
    h = jnp.concatenate(hs, axis=0)
    o_ref[...] = (h * jax.nn.gelu(y_ref[...].astype(F32))).astype(o_ref.dtype)


def rg_lru_block(proj, conv_w, conv_b, wa, ba, wx, bx, lam, batch, seq, col_x):
    t = proj.shape[0]
    w = LRU_WIDTH
    tt = min(256, seq)
    nt = seq // tt
    b0 = col_x // w
    blocks = 2 * 3 * _nbytes((tt, w), BF16) + 24 * _nbytes((tt, w), F32)
    blocks += 4 * _nbytes((LRU_BLOCKS, LRU_BLOCK_DIM, LRU_BLOCK_DIM), BF16)
    kernel = functools.partial(_lru_kernel, tt=tt)
    vec = pl.BlockSpec((1, w), lambda b, n: (0, 0))
    blockdiag = pl.BlockSpec((LRU_BLOCKS, LRU_BLOCK_DIM, LRU_BLOCK_DIM), lambda b, n: (0, 0, 0))
    return pl.pallas_call(
        kernel,
        grid=(batch, nt),
        in_specs=[pl.BlockSpec((tt, w), lambda b, n: (b * nt + n, b0)),
                  pl.BlockSpec((tt, w), lambda b, n: (b * nt + n, b0 + 1)),
                  pl.BlockSpec((LRU_CONV, w), lambda b, n: (0, 0)),
                  vec, blockdiag, vec, blockdiag, vec, vec],
        out_specs=pl.BlockSpec((tt, w), lambda b, n: (b * nt + n, 0)),
        out_shape=jax.ShapeDtypeStruct((t, w), BF16),
        scratch_shapes=[pltpu.VMEM((SUBLANES, w), F32), pltpu.VMEM((1, w), F32)],
        compiler_params=_compiler_params(("arbitrary", "arbitrary"), blocks),
        name="rg_lru",
    )(proj, proj, conv_w, conv_b.reshape(1, w), wa.astype(BF16), ba.reshape(1, w),
      wx.astype(BF16), bx.reshape(1, w), lam.reshape(1, w))


def _merge_kernel(oa_ref, ob_ref, oc_ref, wa_ref, wb_ref, wc_ref, ga_ref, gb_ref, gc_ref, o_ref):
    acc = jax.nn.sigmoid(ga_ref[...].astype(F32)) * _dot(oa_ref[...], wa_ref[...])
    acc = acc + jax.nn.sigmoid(gb_ref[...].astype(F32)) * _dot(ob_ref[...], wb_ref[...])
    acc = acc + jax.nn.sigmoid(gc_ref[...].astype(F32)) * _dot(oc_ref[...], wc_ref[...])
    o_ref[...] = acc.astype(o_ref.dtype)


def branch_merge(o_a, o_b, o_c, w_a, w_b, w_c, proj, col_gates, d_model):
    t, bw = o_a.shape
    tm = min(512, t)
    tn = 1024
    g0 = col_gates // tn
    per_g = d_model // tn
    blocks = 2 * (3 * _nbytes((tm, bw), BF16) + 3 * _nbytes((bw, tn), BF16) + 4 * _nbytes((tm, tn), BF16))
    blocks += 4 * _nbytes((tm, tn), F32)
    o_spec = pl.BlockSpec((tm, bw), lambda i, j: (i, 0))
    w_spec = pl.BlockSpec((bw, tn), lambda i, j: (0, j))

    def gate_spec(idx):
        return pl.BlockSpec((tm, tn), lambda i, j: (i, g0 + idx * per_g + j))

    return pl.pallas_call(
        _merge_kernel,
        grid=(t // tm, d_model // tn),
        in_specs=[o_spec, o_spec, o_spec, w_spec, w_spec, w_spec,
                  gate_spec(0), gate_spec(1), gate_spec(2)],
        out_specs=pl.BlockSpec((tm, tn), lambda i, j: (i, j)),
        out_shape=jax.ShapeDtypeStruct((t, d_model), BF16),
        compiler_params=_compiler_params(("parallel", "arbitrary"), blocks),
        name="branch_merge",
    )(o_a, o_b, o_c, w_a, w_b, w_c, proj, proj, proj)


def _proj_res_norm_kernel(a_ref, w_ref, x_ref, g_ref, xo_ref, ho_ref, acc_ref, *, nk):
    k = pl.program_id(1)
    part = _dot(a_ref[...], w_ref[...])

    @pl.when(k == 0)
    def _():
        acc_ref[...] = part

    @pl.when(k > 0)
    def _():
        acc_ref[...] += part

    @pl.when(k == nk - 1)
    def _():
        xn = x_ref[...] + acc_ref[...]
        xo_ref[...] = xn
        ho_ref[...] = _rms(xn, g_ref[...]).astype(ho_ref.dtype)


def proj_res_norm(a, w, x, g, tk, h_dtype, name):
    t, kdim = a.shape
    d = w.shape[1]
    tm = min(512, t)
    nk = kdim // tk
    blocks = 2 * (_nbytes((tm, tk), BF16) + _nbytes((tk, d), BF16) + 2 * _nbytes((tm, d), F32)
                  + _nbytes((tm, d), h_dtype)) + 3 * _nbytes((tm, d), F32)
    kernel = functools.partial(_proj_res_norm_kernel, nk=nk)
    return pl.pallas_call(
        kernel,
        grid=(t // tm, nk),
        in_specs=[pl.BlockSpec((tm, tk), lambda i, k: (i, k)),
                  pl.BlockSpec((tk, d), lambda i, k: (k, 0)),
                  pl.BlockSpec((tm, d), lambda i, k: (i, 0)),
                  pl.BlockSpec((1, d), lambda i, k: (0, 0))],
        out_specs=[pl.BlockSpec((tm, d), lambda i, k: (i, 0)),
                   pl.BlockSpec((tm, d), lambda i, k: (i, 0))],
        out_shape=[jax.ShapeDtypeStruct((t, d), F32), jax.ShapeDtypeStruct((t, d), h_dtype)],
        scratch_shapes=[pltpu.VMEM((tm, d), F32)],
        compiler_params=_compiler_params(("parallel", "arbitrary"), blocks),
        name=name,
    )(a, w, x, g.reshape(1, d))


def _xattn_kernel(h_ref, wq_ref, k_ref, v_ref, o_ref, *, head_dim):
    q = _dot(h_ref[...], wq_ref[...]).astype(BF16)
    scale = head_dim ** -0.5
    for hd in range(XATTN_HEADS):
        sl = slice(hd * head_dim, (hd + 1) * head_dim)
        s = _dot_nt(q[:, sl], k_ref[:, sl]) * scale
        m = jnp.max(s, axis=-1, keepdims=True)
        e = jnp.exp(s - m)
        p = (e * (1.0 / jnp.sum(e, axis=-1, keepdims=True))).astype(BF16)
        o_ref[:, sl] = _dot(p, v_ref[:, sl]).astype(o_ref.dtype)


def cross_attention(h, wq, kv, seq, mem_len):
    t, d = h.shape
    tm = min(512, seq)
    tiles_per_seq = seq // tm
    blocks = 2 * (2 * _nbytes((tm, d), BF16) + _nbytes((d, d), BF16) + 2 * _nbytes((mem_len, d), BF16))
    blocks += 3 * _nbytes((tm, d), F32)
    kernel = functools.partial(_xattn_kernel, head_dim=d // XATTN_HEADS)
    return pl.pallas_call(
        kernel,
        grid=(t // tm,),
        in_specs=[pl.BlockSpec((tm, d), lambda i: (i, 0)),
                  pl.BlockSpec((d, d), lambda i: (0, 0)),
                  pl.BlockSpec((mem_len, d), lambda i: (i // tiles_per_seq, 0)),
                  pl.BlockSpec((mem_len, d), lambda i: (i // tiles_per_seq, 1))],
        out_specs=pl.BlockSpec((tm, d), lambda i: (i, 0)),
        out_shape=jax.ShapeDtypeStruct((t, d), BF16),
        compiler_params=_compiler_params(("parallel",), blocks),
        name="cross_attention",
    )(h, wq, kv, kv)


def _ffn_up_kernel(h_ref, wg_ref, wv_ref, cw_ref, cb_ref, o_ref, halo_ref, *, tm, tiles_per_seq):
    @pl.when(pl.program_id(1) % tiles_per_seq == 0)
    def _():
        halo_ref[...] = jnp.zeros_like(halo_ref)

    h = h_ref[...]
    g = _dot(h, wg_ref[...])
    v = _dot(h, wv_ref[...])
    ge = jnp.concatenate([halo_ref[...], g], axis=0)
    halo_ref[...] = g[tm - SUBLANES:, :]
    cw = cw_ref[...]
    conv = cb_ref[...] + cw[FFN_CONV - 1:FFN_CONV, :] * g
    for kk in range(FFN_CONV - 1):
        off = SUBLANES - (FFN_CONV - 1) + kk
        conv = conv + cw[kk:kk + 1, :] * ge[off:off + tm, :]
    o_ref[...] = (jax.nn.silu(conv) * v).astype(o_ref.dtype)


def ffn_up(h, w_gate, w_val, conv_w, conv_b, seq):
    t, d = h.shape
    n = w_gate.shape[1]
    tn = FFN_TILE_N
    tm = min(1024, seq)
    blocks = 2 * (_nbytes((tm, d), BF16) + 2 * _nbytes((d, tn), BF16) + _nbytes((tm, tn), BF16))
    blocks += 8 * _nbytes((tm, tn), F32)
    kernel = functools.partial(_ffn_up_kernel, tm=tm, tiles_per_seq=seq // tm)
    return pl.pallas_call(
        kernel,
        grid=(n // tn, t // tm),
        in_specs=[pl.BlockSpec((tm, d), lambda j, i: (i, 0)),
                  pl.BlockSpec((d, tn), lambda j, i: (0, j)),
                  pl.BlockSpec((d, tn), lambda j, i: (0, j)),
                  pl.BlockSpec((FFN_CONV, tn), lambda j, i: (0, j)),
                  pl.BlockSpec((1, tn), lambda j, i: (0, j))],
        out_specs=pl.BlockSpec((tm, tn), lambda j, i: (i, j)),
        out_shape=jax.ShapeDtypeStruct((t, n), BF16),
        scratch_shapes=[pltpu.VMEM((SUBLANES, tn), F32)],
        compiler_params=_compiler_params(("arbitrary", "arbitrary"), blocks),
        name="ffn_up",
    )(h, w_gate, w_val, conv_w, conv_b.reshape(1, n))


def _pad_cols(a, n):
    return jnp.pad(a, [(0, 0)] * (a.ndim - 1) + [(0, n - a.shape[-1])])


def kernel(x, mem, mem_norm_g, norm_mix_g, w_in, attn_sinks, lru_conv_w, lru_conv_b, lru_wa, lru_ba,
           lru_wx, lru_bx, lru_lambda, w_branch, w_out, norm_xattn_g, xattn_wq, xattn_wkv, xattn_wo,
           norm_ffn_g, ffn_w_up, ffn_conv_w, ffn_conv_b, ffn_w_down, final_norm_g):
    batch, seq, d_model = x.shape
    depth = w_in.shape[0]
    mem_len = mem.shape[1]
    ffn_dim = ffn_w_down.shape[1]
    t = batch * seq
    assert seq % 512 == 0 or seq in (128, 256)
    assert d_model == 2048 and w_in.shape[2] == 13824

    perm_heads = np.asarray(_q_head_order())
    q_cols = (perm_heads[:, None] * SWA_HEAD_DIM + np.arange(SWA_HEAD_DIM)[None, :]).reshape(-1)
    o_q, o_k, o_v = 0, SWA_Q_W, SWA_Q_W + SWA_KV_W
    o_ret = o_v + SWA_KV_W
    o_lru = o_ret + 4 * RET_W
    o_gate = o_lru + 2 * LRU_WIDTH
    col_ret, col_lru, col_gate = 0, 4 * RET_W, 4 * RET_W + 2 * LRU_WIDTH
    col_q = col_gate + 3 * d_model
    col_k = col_q + SWA_Q_W
    col_v = col_k + SWA_KV_W

    ffn_pad = -(-ffn_dim // FFN_TILE_N) * FFN_TILE_N
    cosf, sinf = _rope_tables(seq)

    xf = x.reshape(t, d_model)
    mem_n = rmsnorm(mem.reshape(batch * mem_len, d_model), mem_norm_g)
    h = rmsnorm(xf, norm_mix_g[0])
    out = None
    for l in range(depth):
        wl = w_in[l]
        w_in_l = jnp.concatenate(
            [wl[:, o_ret:o_gate + 3 * d_model], wl[:, o_q:o_k][:, q_cols], wl[:, o_k:o_ret]],
            axis=1).astype(BF16)
        proj = matmul(h, w_in_l, 1024, 1536, "in_proj")

        o_a = swa_attention(proj, attn_sinks[l], seq, col_q, col_k, col_v)
        o_b = retention(proj, cosf, sinf, batch, seq, col_ret)
        o_c = rg_lru_block(proj, lru_conv_w[l], lru_conv_b[l], lru_wa[l], lru_ba[l], lru_wx[l],
                           lru_bx[l], lru_lambda[l], batch, seq, col_lru)
        wb = w_branch[l].astype(BF16)
        merged = branch_merge(o_a, o_b, o_c, wb[0][q_cols, :], wb[1], wb[2], proj, col_gate, d_model)
        xf, h = proj_res_norm(merged, w_out[l].astype(BF16), xf, norm_xattn_g[l], d_model, BF16,
                              "mixer_out")

        kv = matmul(mem_n, xattn_wkv[l].astype(BF16), batch * mem_len, 1024, "xattn_kv")
        o_x = cross_attention(h, xattn_wq[l].astype(BF16), kv, seq, mem_len)
        xf, h = proj_res_norm(o_x, xattn_wo[l].astype(BF16), xf, norm_ffn_g[l], d_model, BF16,
                              "xattn_out")

        w_gate = _pad_cols(ffn_w_up[l][:, :ffn_dim], ffn_pad).astype(BF16)
        w_val = _pad_cols(ffn_w_up[l][:, ffn_dim:], ffn_pad).astype(BF16)
        act = ffn_up(h, w_gate, w_val, _pad_cols(ffn_conv_w[l], ffn_pad),
                     _pad_cols(ffn_conv_b[l], ffn_pad), seq)
        w_down = jnp.pad(ffn_w_down[l], ((0, ffn_pad - ffn_dim), (0, 0))).astype(BF16)
        last = l == depth - 1
        g_next = final_norm_g if last else norm_mix_g[l + 1]
        xf, h = proj_res_norm(act, w_down, xf, g_next, ffn_pad // 4, F32 if last else BF16, "ffn_down")
        out = h
    return out.reshape(batch, seq, d_model)
```

```python
import functools

import numpy as np
import jax
import jax.numpy as jnp
from jax import lax
from jax.experimental import pallas as pl
from jax.experimental.pallas import tpu as pltpu

NORM_EPS = 1e-6

LANES = 128
SUBLANES = 8
BF16_ROWS = 16
VMEM_BUDGET_BYTES = 58 * 1024 * 1024

SWA_Q_HEADS = 16
SWA_KV_HEADS = 4
SWA_HEAD_DIM = 64
SWA_BLOCK = 128
SWA_Q_W = SWA_Q_HEADS * SWA_HEAD_DIM
SWA_KV_W = SWA_KV_HEADS * SWA_HEAD_DIM

RET_HEADS = 8
RET_HEAD_DIM = 128
RET_CHUNK = 128
RET_W = RET_HEADS * RET_HEAD_DIM
RET_ROPE_BASE = 10000.0

LRU_WIDTH = 1024
LRU_BLOCKS = 8
LRU_BLOCK_DIM = LRU_WIDTH // LRU_BLOCKS
LRU_CONV = 4
LRU_C = 8.0

XATTN_HEADS = 4

FFN_CONV = 3
FFN_TILE_N = 512
IN_TILE_N = 512

LOG2E = 1.4426950408889634

BF16 = jnp.bfloat16
F32 = jnp.float32


def _compiler_params(semantics, vmem_bytes):
    return pltpu.CompilerParams(
        dimension_semantics=semantics,
        vmem_limit_bytes=int(min(max(vmem_bytes, 16 * 1024 * 1024), VMEM_BUDGET_BYTES)))


def _nbytes(shape, dtype):
    return int(np.prod(shape)) * jnp.dtype(dtype).itemsize


def _dot(a, b):
    return jnp.dot(a, b, preferred_element_type=F32)


def _dot_nt(a, b):
    return lax.dot_general(a, b, (((1,), (1,)), ((), ())), preferred_element_type=F32)


def _rms(x, g):
    ms = jnp.mean(x * x, axis=-1, keepdims=True)
    return x * lax.rsqrt(ms + NORM_EPS) * g


def _rmsnorm_kernel(x_ref, g_ref, o_ref):
    o_ref[...] = _rms(x_ref[...], g_ref[...]).astype(o_ref.dtype)


def rmsnorm(x, g, out_dtype=BF16):
    t, d = x.shape
    tm = min(512, t)
    blocks = 2 * (_nbytes((tm, d), F32) + _nbytes((tm, d), out_dtype)) + 2 * _nbytes((tm, d), F32)
    return pl.pallas_call(
        _rmsnorm_kernel,
        grid=(t // tm,),
        in_specs=[pl.BlockSpec((tm, d), lambda i: (i, 0)),
                  pl.BlockSpec((1, d), lambda i: (0, 0))],
        out_specs=pl.BlockSpec((tm, d), lambda i: (i, 0)),
        out_shape=jax.ShapeDtypeStruct((t, d), out_dtype),
        compiler_params=_compiler_params(("parallel",), blocks),
        name="rmsnorm",
    )(x, g.reshape(1, d))


def _matmul_kernel(a_ref, w_ref, o_ref):
    o_ref[...] = _dot(a_ref[...], w_ref[...]).astype(o_ref.dtype)


def matmul(a, w, layer, tm, tn, name):
    m, k = a.shape
    n = w.shape[2]
    tm = min(tm, m)
    blocks = (2 * (_nbytes((tm, k), BF16) + _nbytes((k, tn), BF16) + _nbytes((tm, tn), BF16))
              + 2 * _nbytes((tm, tn), F32))
    return pl.pallas_call(
        _matmul_kernel,
        grid=(m // tm, n // tn),
        in_specs=[pl.BlockSpec((tm, k), lambda i, j: (i, 0)),
                  pl.BlockSpec((None, k, tn), lambda i, j: (layer, 0, j))],
        out_specs=pl.BlockSpec((tm, tn), lambda i, j: (i, j)),
        out_shape=jax.ShapeDtypeStruct((m, n), BF16),
        compiler_params=_compiler_params(("parallel", "arbitrary"), blocks),
        name=name,
    )(a, w)


def _in_proj_kernel(h_ref, wq_ref, w_ref, o_ref, *, q_tiles):
    j = pl.program_id(1)

    @pl.when(j < q_tiles)
    def _():
        o_ref[...] = _dot(h_ref[...], wq_ref[...]).astype(o_ref.dtype)

    @pl.when(j >= q_tiles)
    def _():
        o_ref[...] = _dot(h_ref[...], w_ref[...]).astype(o_ref.dtype)


def in_proj(h, wq_perm, w_in, layer):
    m, k = h.shape
    n = w_in.shape[2]
    tn = IN_TILE_N
    tm = min(2048, m)
    nt = n // tn
    q_tiles = SWA_Q_W // tn
    head = (SWA_Q_W + 2 * SWA_KV_W) // tn
    blocks = (2 * (_nbytes((tm, k), BF16) + 2 * _nbytes((k, tn), BF16) + _nbytes((tm, tn), BF16))
              + 2 * _nbytes((tm, tn), F32))
    kernel = functools.partial(_in_proj_kernel, q_tiles=q_tiles)
    return pl.pallas_call(
        kernel,
        grid=(m // tm, nt),
        in_specs=[pl.BlockSpec((tm, k), lambda i, j: (i, 0)),
                  pl.BlockSpec((None, k, tn), lambda i, j: (layer, 0, jnp.minimum(j, q_tiles - 1))),
                  pl.BlockSpec((None, k, tn), lambda i, j: (layer, 0, jnp.maximum(j, q_tiles)))],
        out_specs=pl.BlockSpec((tm, tn), lambda i, j: (i, (j + nt - head) % nt)),
        out_shape=jax.ShapeDtypeStruct((m, n), BF16),
        compiler_params=_compiler_params(("parallel", "arbitrary"), blocks),
        name="in_proj",
    )(h, wq_perm, w_in)


def _q_head_order():
    order = []
    g = SWA_Q_HEADS // SWA_KV_HEADS
    for kv_pair in range(SWA_KV_HEADS // 2):
        for i in range(g):
            order += [(2 * kv_pair) * g + i, (2 * kv_pair + 1) * g + i]
    return order


def _swa_kernel(sinks_ref, q_ref, kc_ref, vc_ref, kp_ref, vp_ref, o_ref, *, tq, tiles_per_seq):
    i = pl.program_id(0)
    starts_seq = (i % tiles_per_seq) == 0
    blk = SWA_BLOCK
    kwin = jnp.concatenate([kp_ref[...], kc_ref[...]], axis=0)
    vwin = jnp.concatenate([vp_ref[...], vc_ref[...]], axis=0)
    lane = lax.broadcasted_iota(jnp.int32, (blk, LANES), 1)
    lo = lane < SWA_HEAD_DIM
    row = lax.broadcasted_iota(jnp.int32, (2 * blk, 2 * blk), 0)
    col = lax.broadcasted_iota(jnp.int32, (2 * blk, 2 * blk), 1)
    qi = jnp.bitwise_and(row, blk - 1)
    in_window = jnp.logical_and(col > qi, col <= qi + blk)
    in_window_first = jnp.logical_and(in_window, jnp.logical_or(col >= blk, jnp.logical_not(starts_seq)))
    top = lax.broadcasted_iota(jnp.int32, (2 * blk, 1), 0) < blk
    scale = (SWA_HEAD_DIM ** -0.5) * LOG2E
    order = _q_head_order()
    for c in range(tq // blk):
        allowed = in_window_first if c == 0 else in_window
        for p in range(SWA_Q_W // LANES):
            kv_blk = p // (SWA_Q_W // LANES // (SWA_KV_W // LANES))
            kb = kwin[c * blk:(c + 2) * blk, kv_blk * LANES:(kv_blk + 1) * LANES]
            vb = vwin[c * blk:(c + 2) * blk, kv_blk * LANES:(kv_blk + 1) * LANES]
            qb = q_ref[c * blk:(c + 1) * blk, p * LANES:(p + 1) * LANES]
            zero = jnp.zeros_like(qb)
            qs = jnp.concatenate([jnp.where(lo, qb, zero), jnp.where(lo, zero, qb)], axis=0)
            s = _dot_nt(qs, kb) * scale
            s = jnp.where(allowed, s, -jnp.inf)
            sink = jnp.where(top, sinks_ref[order[2 * p]], sinks_ref[order[2 * p + 1]]) * LOG2E
            m = jnp.maximum(jnp.max(s, axis=-1, keepdims=True), sink)
            e = jnp.exp2(s - m)
            denom = jnp.sum(e, axis=-1, keepdims=True) + jnp.exp2(sink - m)
            probs = (e * (1.0 / denom)).astype(BF16)
            o = _dot(probs, vb)
            ob = jnp.where(lo, o[:blk], o[blk:])
            o_ref[c * blk:(c + 1) * blk, p * LANES:(p + 1) * LANES] = ob.astype(o_ref.dtype)


def swa_attention(proj, sinks, seq, col_q, col_k, col_v):
    t = proj.shape[0]
    tq = min(512, seq)
    nsub = tq // SWA_BLOCK
    qb, kb, vb = col_q // SWA_Q_W, col_k // SWA_KV_W, col_v // SWA_KV_W
    blocks = 2 * (2 * _nbytes((tq, SWA_Q_W), BF16) + 2 * _nbytes((tq + SWA_BLOCK, SWA_KV_W), BF16))
    blocks += 16 * _nbytes((2 * SWA_BLOCK, 2 * SWA_BLOCK), F32)
    kernel = functools.partial(_swa_kernel, tq=tq, tiles_per_seq=seq // tq)
    return pl.pallas_call(
        kernel,
        grid=(t // tq,),
        in_specs=[
            pl.BlockSpec(memory_space=pltpu.SMEM),
            pl.BlockSpec((tq, SWA_Q_W), lambda i: (i, qb)),
            pl.BlockSpec((tq, SWA_KV_W), lambda i: (i, kb)),
            pl.BlockSpec((tq, SWA_KV_W), lambda i: (i, vb)),
            pl.BlockSpec((SWA_BLOCK, SWA_KV_W), lambda i: (jnp.maximum(i * nsub - 1, 0), kb)),
            pl.BlockSpec((SWA_BLOCK, SWA_KV_W), lambda i: (jnp.maximum(i * nsub - 1, 0), vb)),
        ],
        out_specs=pl.BlockSpec((tq, SWA_Q_W), lambda i: (i, 0)),
        out_shape=jax.ShapeDtypeStruct((t, SWA_Q_W), BF16),
        compiler_params=_compiler_params(("parallel",), blocks),
        name="swa_attention",
    )(sinks, proj, proj, proj, proj, proj)


_RET_LOG_GAMMA = [float(v) for v in
                  np.log1p(-np.exp2(-5.0 - np.arange(RET_HEADS, dtype=np.float32))).astype(np.float32)]


def _retention_kernel(q_ref, k_ref, v_ref, g_ref, cos_ref, sin_ref, o_ref, state_ref, decay_ref, *,
                      chunks):
    c = RET_CHUNK
    d = RET_HEAD_DIM

    @pl.when(pl.program_id(1) == 0)
    def _():
        state_ref[...] = jnp.zeros_like(state_ref)
        qpos = lax.broadcasted_iota(jnp.int32, (c, c), 0)
        kpos = lax.broadcasted_iota(jnp.int32, (c, c), 1)
        diff = (qpos - kpos).astype(F32)
        for h in range(RET_HEADS):
            decay_ref[h] = jnp.where(diff >= 0, jnp.exp(jnp.maximum(diff, 0.0) * _RET_LOG_GAMMA[h]), 0.0)

    idx = lax.broadcasted_iota(jnp.int32, (c, 1), 0).astype(F32)
    for h in range(RET_HEADS):
        lg = _RET_LOG_GAMMA[h]
        sl = slice(h * d, (h + 1) * d)
        q_decay = jnp.exp((idx + 1.0) * lg)
        k_decay = jnp.exp((c - 1.0 - idx) * lg)
        chunk_decay = float(np.exp(np.float32(c) * np.float32(lg)))
        state = state_ref[h]
        for n in range(chunks):
            rows = slice(n * c, (n + 1) * c)
            cosf = cos_ref[rows, :]
            sinf = sin_ref[rows, :]
            q = q_ref[rows, sl].astype(F32)
            k = k_ref[rows, sl].astype(F32)
            v = v_ref[rows, sl]
            qr = q * cosf + pltpu.roll(q, d // 2, axis=1) * sinf
            kr = (k * cosf + pltpu.roll(k, d // 2, axis=1) * sinf) * (d ** -0.5)
            scores = _dot_nt(qr.astype(BF16), kr.astype(BF16)) * decay_ref[h]
            intra = _dot(scores.astype(BF16), v)
            inter = _dot((qr * q_decay).astype(BF16), state.astype(BF16))
            kd_t = (kr * k_decay).T.astype(BF16)
            state = state * chunk_decay + _dot(kd_t, v)
            y = intra + inter
            mu = jnp.mean(y, axis=-1, keepdims=True)
            yc = y - mu
            yn = yc * lax.rsqrt(jnp.mean(yc * yc, axis=-1, keepdims=True) + NORM_EPS)
            gate = g_ref[rows, sl].astype(F32)
            o_ref[rows, sl] = (jax.nn.silu(gate) * yn).astype(o_ref.dtype)
        state_ref[h] = state


def _rope_tables(seq):
    half = RET_HEAD_DIM // 2
    theta = RET_ROPE_BASE ** (-jnp.linspace(0.0, 1.0, half, dtype=F32))
    ang = jnp.arange(seq, dtype=F32)[:, None] * theta[None, :]
    cos = jnp.cos(ang)
    sin = jnp.sin(ang)
    return jnp.concatenate([cos, cos], axis=-1), jnp.concatenate([-sin, sin], axis=-1)


def retention(proj, cosf, sinf, batch, seq, col_q):
    t = proj.shape[0]
    c = RET_CHUNK
    tr = min(4 * c, seq)
    nt = seq // tr
    b0 = col_q // RET_W
    blocks = 2 * 5 * _nbytes((tr, RET_W), BF16) + 4 * _nbytes((tr, RET_HEAD_DIM), F32)
    blocks += 2 * _nbytes((RET_HEADS, RET_HEAD_DIM, RET_HEAD_DIM), F32) + 96 * _nbytes((c, c), F32)
    kernel = functools.partial(_retention_kernel, chunks=tr // c)

    def col(j):
        return pl.BlockSpec((tr, RET_W), lambda b, n: (b * nt + n, b0 + j))

    return pl.pallas_call(
        kernel,
        grid=(batch, nt),
        in_specs=[col(0), col(1), col(2), col(3),
                  pl.BlockSpec((tr, RET_HEAD_DIM), lambda b, n: (n, 0)),
                  pl.BlockSpec((tr, RET_HEAD_DIM), lambda b, n: (n, 0))],
        out_specs=pl.BlockSpec((tr, RET_W), lambda b, n: (b * nt + n, 0)),
        out_shape=jax.ShapeDtypeStruct((t, RET_W), BF16),
        scratch_shapes=[pltpu.VMEM((RET_HEADS, RET_HEAD_DIM, RET_HEAD_DIM), F32),
                        pltpu.VMEM((RET_HEADS, c, c), F32)],
        compiler_params=_compiler_params(("arbitrary", "arbitrary"), blocks),
        name="retention",
    )(proj, proj, proj, proj, cosf, sinf)


def _lru_kernel(x_ref, y_ref, cw_ref, cb_ref, wa_ref, ba_ref, wx_ref, bx_ref, lam_ref, o_ref,
                xprev_ref, carry_ref, *, tt):
    w = LRU_WIDTH

    @pl.when(pl.program_id(1) == 0)
    def _():
        xprev_ref[...] = jnp.zeros_like(xprev_ref)
        carry_ref[...] = jnp.zeros_like(carry_ref)

    x = x_ref[...].astype(F32)
    xe = jnp.concatenate([xprev_ref[...], x], axis=0)
    xprev_ref[...] = x[tt - SUBLANES:, :]
    cw = cw_ref[...]
    xc = cb_ref[...] + cw[3:4, :] * x
    for kk in range(LRU_CONV - 1):
        off = SUBLANES - (LRU_CONV - 1) + kk
        xc = xc + cw[kk:kk + 1, :] * xe[off:off + tt, :]

    xcb = xc.astype(BF16)
    gr = []
    gi = []
    for blk in range(LRU_BLOCKS):
        sl = slice(blk * LRU_BLOCK_DIM, (blk + 1) * LRU_BLOCK_DIM)
        gr.append(_dot(xcb[:, sl], wa_ref[blk]))
        gi.append(_dot(xcb[:, sl], wx_ref[blk]))
    gate_r = jax.nn.sigmoid(jnp.concatenate(gr, axis=1) + ba_ref[...])
    gate_i = jax.nn.sigmoid(jnp.concatenate(gi, axis=1) + bx_ref[...])
    z = -lam_ref[...]
    softplus = jnp.maximum(z, 0.0) + jnp.log1p(jnp.exp(-jnp.abs(z)))
    log_a = (-LRU_C) * gate_r * softplus
    a = jnp.exp(log_a)
    u = jnp.sqrt(1.0 - a * a) * (gate_i * xc)

    nv = tt // SUBLANES
    a3 = a.reshape(nv, SUBLANES, w)
    b3 = u.reshape(nv, SUBLANES, w)
    rix = lax.broadcasted_iota(jnp.int32, (nv, SUBLANES, w), 1)
    s = 1
    while s < SUBLANES:
        keep = rix >= s
        a_sh = jnp.where(keep, pltpu.roll(a3, s, axis=1), 1.0)
        b_sh = jnp.where(keep, pltpu.roll(b3, s, axis=1), 0.0)
        b3 = a3 * b_sh + b3
        a3 = a3 * a_sh
        s *= 2
    carry = carry_ref[...]
    hs = []
    for i in range(nv):
        h_i = a3[i] * carry + b3[i]
        carry = h_i[SUBLANES - 1:SUBLANES, :]
        hs.append(h_i)
    carry_ref[...] = carry
    h = jnp.concatenate(hs, axis=0)
    o_ref[...] = (h * jax.nn.gelu(y_ref[...].astype(F32))).astype(o_ref.dtype)


def rg_lru_block(proj, conv_w, conv_b, wa, ba, wx, bx, lam, layer, batch, seq, col_x):
    t = proj.shape[0]
    w = LRU_WIDTH
    tt = min(256, seq)
    nt = seq // tt
    b0 = col_x // w
    blocks = 2 * 3 * _nbytes((tt, w), BF16) + 24 * _nbytes((tt, w), F32)
    blocks += 4 * _nbytes((LRU_BLOCKS, LRU_BLOCK_DIM, LRU_BLOCK_DIM), BF16)
    kernel = functools.partial(_lru_kernel, tt=tt)
    vec = pl.BlockSpec((1, w), lambda b, n: (0, 0))
    blockdiag = pl.BlockSpec((None, LRU_BLOCKS, LRU_BLOCK_DIM, LRU_BLOCK_DIM),
                             lambda b, n: (layer, 0, 0, 0))
    return pl.pallas_call(
        kernel,
        grid=(batch, nt),
        in_specs=[pl.BlockSpec((tt, w), lambda b, n: (b * nt + n, b0)),
                  pl.BlockSpec((tt, w), lambda b, n: (b * nt + n, b0 + 1)),
                  pl.BlockSpec((LRU_CONV, w), lambda b, n: (0, 0)),
                  vec, blockdiag, vec, blockdiag, vec, vec],
        out_specs=pl.BlockSpec((tt, w), lambda b, n: (b * nt + n, 0)),
        out_shape=jax.ShapeDtypeStruct((t, w), BF16),
        scratch_shapes=[pltpu.VMEM((SUBLANES, w), F32), pltpu.VMEM((1, w), F32)],
        compiler_params=_compiler_params(("arbitrary", "arbitrary"), blocks),
        name="rg_lru",
    )(proj, proj, conv_w, conv_b.reshape(1, w), wa, ba.reshape(1, w), wx, bx.reshape(1, w),
      lam.reshape(1, w))


def _merge_kernel(oa_ref, ob_ref, oc_ref, wa_ref, wb_ref, wc_ref, ga_ref, gb_ref, gc_ref, o_ref):
    acc = jax.nn.sigmoid(ga_ref[...].astype(F32)) * _dot(oa_ref[...], wa_ref[...])
    acc = acc + jax.nn.sigmoid(gb_ref[...].astype(F32)) * _dot(ob_ref[...], wb_ref[...])
    acc = acc + jax.nn.sigmoid(gc_ref[...].astype(F32)) * _dot(oc_ref[...], wc_ref[...])
    o_ref[...] = acc.astype(o_ref.dtype)


def branch_merge(o_a, o_b, o_c, w_a_perm, w_branch, layer, proj, col_gates, d_model):
    t, bw = o_a.shape
    tm = min(512, t)
    tn = 1024
    g0 = col_gates // tn
    per_g = d_model // tn
    blocks = 2 * (3 * _nbytes((tm, bw), BF16) + 3 * _nbytes((bw, tn), BF16) + 4 * _nbytes((tm, tn), BF16))
    blocks += 4 * _nbytes((tm, tn), F32)
    o_spec = pl.BlockSpec((tm, bw), lambda i, j: (i, 0))

    def w_spec(idx):
        return pl.BlockSpec((None, None, bw, tn), lambda i, j: (layer, idx, 0, j))

    def gate_spec(idx):
        return pl.BlockSpec((tm, tn), lambda i, j: (i, g0 + idx * per_g + j))

    return pl.pallas_call(
        _merge_kernel,
        grid=(t // tm, d_model // tn),
        in_specs=[o_spec, o_spec, o_spec,
                  pl.BlockSpec((None, bw, tn), lambda i, j: (layer, 0, j)), w_spec(1), w_spec(2),
                  gate_spec(0), gate_spec(1), gate_spec(2)],
        out_specs=pl.BlockSpec((tm, tn), lambda i, j: (i, j)),
        out_shape=jax.ShapeDtypeStruct((t, d_model), BF16),
        compiler_params=_compiler_params(("parallel", "arbitrary"), blocks),
        name="branch_merge",
    )(o_a, o_b, o_c, w_a_perm, w_branch, w_branch, proj, proj, proj)


def _proj_res_norm_kernel(a_ref, w_ref, x_ref, g_ref, xo_ref, ho_ref):
    xn = x_ref[...] + _dot(a_ref[...], w_ref[...])
    xo_ref[...] = xn
    ho_ref[...] = _rms(xn, g_ref[...]).astype(ho_ref.dtype)


def proj_res_norm(a, w, layer, x, g, name):
    t, kdim = a.shape
    d = w.shape[2]
    tm = min(512, t)
    blocks = 2 * (_nbytes((tm, kdim), BF16) + _nbytes((kdim, d), BF16) + 2 * _nbytes((tm, d), F32)
                  + _nbytes((tm, d), BF16)) + 3 * _nbytes((tm, d), F32)
    return pl.pallas_call(
        _proj_res_norm_kernel,
        grid=(t // tm,),
        in_specs=[pl.BlockSpec((tm, kdim), lambda i: (i, 0)),
                  pl.BlockSpec((None, kdim, d), lambda i: (layer, 0, 0)),
                  pl.BlockSpec((tm, d), lambda i: (i, 0)),
                  pl.BlockSpec((1, d), lambda i: (0, 0))],
        out_specs=[pl.BlockSpec((tm, d), lambda i: (i, 0)),
                   pl.BlockSpec((tm, d), lambda i: (i, 0))],
        out_shape=[jax.ShapeDtypeStruct((t, d), F32), jax.ShapeDtypeStruct((t, d), BF16)],
        compiler_params=_compiler_params(("parallel",), blocks),
        name=name,
    )(a, w, x, g.reshape(1, d))


def _xattn_kernel(h_ref, wq_ref, k_ref, v_ref, o_ref, *, head_dim):
    q = _dot(h_ref[...], wq_ref[...]).astype(BF16)
    scale = (head_dim ** -0.5) * LOG2E
    for hd in range(XATTN_HEADS):
        sl = slice(hd * head_dim, (hd + 1) * head_dim)
        s = _dot_nt(q[:, sl], k_ref[:, sl]) * scale
        m = jnp.max(s, axis=-1, keepdims=True)
        e = jnp.exp2(s - m)
        p = (e * (1.0 / jnp.sum(e, axis=-1, keepdims=True))).astype(BF16)
        o_ref[:, sl] = _dot(p, v_ref[:, sl]).astype(o_ref.dtype)


def cross_attention(h, wq, layer, kv, seq, mem_len):
    t, d = h.shape
    tm = min(512, seq)
    tiles_per_seq = seq // tm
    blocks = 2 * (2 * _nbytes((tm, d), BF16) + _nbytes((d, d), BF16) + 2 * _nbytes((mem_len, d), BF16))
    blocks += 3 * _nbytes((tm, d), F32)
    kernel = functools.partial(_xattn_kernel, head_dim=d // XATTN_HEADS)
    return pl.pallas_call(
        kernel,
        grid=(t // tm,),
        in_specs=[pl.BlockSpec((tm, d), lambda i: (i, 0)),
                  pl.BlockSpec((None, d, d), lambda i: (layer, 0, 0)),
                  pl.BlockSpec((mem_len, d), lambda i: (i // tiles_per_seq, 0)),
                  pl.BlockSpec((mem_len, d), lambda i: (i // tiles_per_seq, 1))],
        out_specs=pl.BlockSpec((tm, d), lambda i: (i, 0)),
        out_shape=jax.ShapeDtypeStruct((t, d), BF16),
        compiler_params=_compiler_params(("parallel",), blocks),
        name="cross_attention",
    )(h, wq, kv, kv)


def _ffn_kernel(h_ref, halo_ref, wg_ref, wv_ref, cw_ref, cb_ref, wd_ref, x_ref, g_ref, xo_ref, ho_ref,
                hext_ref, acc_ref, *, tm, n_tiles, tiles_per_seq):
    i = pl.program_id(0)
    j = pl.program_id(1)
    halo = BF16_ROWS

    @pl.when(j == 0)
    def _():
        keep = (i % tiles_per_seq) != 0
        hext_ref[0:halo, :] = jnp.where(keep, halo_ref[...], jnp.zeros_like(halo_ref))
        hext_ref[halo:, :] = h_ref[...]
        acc_ref[...] = jnp.zeros_like(acc_ref)

    g = _dot(hext_ref[...], wg_ref[...])
    v = _dot(h_ref[...], wv_ref[...])
    cw = cw_ref[...]
    conv = cb_ref[...] + cw[FFN_CONV - 1:FFN_CONV, :] * g[halo:, :]
    for kk in range(FFN_CONV - 1):
        off = halo - (FFN_CONV - 1) + kk
        conv = conv + cw[kk:kk + 1, :] * g[off:off + tm, :]
    act = (jax.nn.silu(conv) * v).astype(BF16)
    acc_ref[...] += _dot(act, wd_ref[...])

    @pl.when(j == n_tiles - 1)
    def _():
        xn = x_ref[...] + acc_ref[...]
        xo_ref[...] = xn
        ho_ref[...] = _rms(xn, g_ref[...]).astype(ho_ref.dtype)


def ffn(h, w_gate, w_val, conv_w, conv_b, w_down, layer, x, g, seq, h_dtype):
    t, d = h.shape
    n = w_gate.shape[2]
    tn = FFN_TILE_N
    tm = min(512, seq)
    n_tiles = n // tn
    halo = BF16_ROWS
    blocks = 2 * (_nbytes((tm, d), BF16) + 2 * _nbytes((d, tn), BF16) + _nbytes((tn, d), BF16)
                  + 2 * _nbytes((tm, d), F32) + _nbytes((tm, d), h_dtype))
    blocks += _nbytes((tm + halo, d), BF16) + _nbytes((tm, d), F32)
    blocks += 5 * _nbytes((tm, tn), F32) + 2 * _nbytes((tm, d), F32)
    kernel = functools.partial(_ffn_kernel, tm=tm, n_tiles=n_tiles, tiles_per_seq=seq // tm)
    return pl.pallas_call(
        kernel,
        grid=(t // tm, n_tiles),
        in_specs=[pl.BlockSpec((tm, d), lambda i, j: (i, 0)),
                  pl.BlockSpec((halo, d), lambda i, j: (jnp.maximum(i * (tm // halo) - 1, 0), 0)),
                  pl.BlockSpec((None, d, tn), lambda i, j: (layer, 0, j)),
                  pl.BlockSpec((None, d, tn), lambda i, j: (layer, 0, j)),
                  pl.BlockSpec((None, FFN_CONV, tn), lambda i, j: (layer, 0, j)),
                  pl.BlockSpec((None, 1, tn), lambda i, j: (layer, 0, j)),
                  pl.BlockSpec((None, tn, d), lambda i, j: (layer, j, 0)),
                  pl.BlockSpec((tm, d), lambda i, j: (i, 0)),
                  pl.BlockSpec((1, d), lambda i, j: (0, 0))],
        out_specs=[pl.BlockSpec((tm, d), lambda i, j: (i, 0)),
                   pl.BlockSpec((tm, d), lambda i, j: (i, 0))],
        out_shape=[jax.ShapeDtypeStruct((t, d), F32), jax.ShapeDtypeStruct((t, d), h_dtype)],
        scratch_shapes=[pltpu.VMEM((tm + halo, d), BF16), pltpu.VMEM((tm, d), F32)],
        compiler_params=_compiler_params(("parallel", "arbitrary"), blocks),
        name="ffn",
    )(h, h, w_gate, w_val, conv_w, conv_b, w_down, x, g.reshape(1, d))


def _pad_last(a, n):
    return jnp.pad(a, [(0, 0)] * (a.ndim - 1) + [(0, n - a.shape[-1])])


def kernel(x, mem, mem_norm_g, norm_mix_g, w_in, attn_sinks, lru_conv_w, lru_conv_b, lru_wa, lru_ba,
           lru_wx, lru_bx, lru_lambda, w_branch, w_out, norm_xattn_g, xattn_wq, xattn_wkv, xattn_wo,
           norm_ffn_g, ffn_w_up, ffn_conv_w, ffn_conv_b, ffn_w_down, final_norm_g):
    batch, seq, d_model = x.shape
    depth = w_in.shape[0]
    mem_len = mem.shape[1]
    ffn_dim = ffn_w_down.shape[1]
    t = batch * seq
    assert seq % 512 == 0 or seq in (128, 256)
    assert d_model == 2048 and w_in.shape[2] == 13824

    col_ret, col_lru, col_gate = 0, 4 * RET_W, 4 * RET_W + 2 * LRU_WIDTH
    col_q = col_gate + 3 * d_model
    col_k = col_q + SWA_Q_W
    col_v = col_k + SWA_KV_W

    perm_heads = np.asarray(_q_head_order())
    q_cols = (perm_heads[:, None] * SWA_HEAD_DIM + np.arange(SWA_HEAD_DIM)[None, :]).reshape(-1)
    w_in_b = w_in.astype(BF16)
    wq_perm = w_in[:, :, :SWA_Q_W][:, :, q_cols].astype(BF16)
    w_branch_b = w_branch.astype(BF16)
    w_branch_a = w_branch[:, 0][:, q_cols, :].astype(BF16)
    w_out_b = w_out.astype(BF16)
    wq_b = xattn_wq.astype(BF16)
    wkv_b = xattn_wkv.astype(BF16)
    wo_b = xattn_wo.astype(BF16)
    lru_wa_b = lru_wa.astype(BF16)
    lru_wx_b = lru_wx.astype(BF16)
    ffn_pad = -(-ffn_dim // FFN_TILE_N) * FFN_TILE_N
    w_gate = _pad_last(ffn_w_up[:, :, :ffn_dim], ffn_pad).astype(BF16)
    w_val = _pad_last(ffn_w_up[:, :, ffn_dim:], ffn_pad).astype(BF16)
    w_down = jnp.pad(ffn_w_down, ((0, 0), (0, ffn_pad - ffn_dim), (0, 0))).astype(BF16)
    conv_w = _pad_last(ffn_conv_w, ffn_pad)
    conv_b = _pad_last(ffn_conv_b, ffn_pad)[:, None, :]
    cosf, sinf = _rope_tables(seq)

    xf = x.reshape(t, d_model)
    mem_n = rmsnorm(mem.reshape(batch * mem_len, d_model), mem_norm_g)
    h = rmsnorm(xf, norm_mix_g[0])
    for l in range(depth):
        proj = in_proj(h, wq_perm, w_in_b, l)
        o_a = swa_attention(proj, attn_sinks[l], seq, col_q, col_k, col_v)
        o_b = retention(proj, cosf, sinf, batch, seq, col_ret)
        o_c = rg_lru_block(proj, lru_conv_w[l], lru_conv_b[l], lru_wa_b, lru_ba[l], lru_wx_b,
                           lru_bx[l], lru_lambda[l], l, batch, seq, col_lru)
        merged = branch_merge(o_a, o_b, o_c, w_branch_a, w_branch_b, l, proj, col_gate, d_model)
        xf, h = proj_res_norm(merged, w_out_b, l, xf, norm_xattn_g[l], "mixer_out")

        kv = matmul(mem_n, wkv_b, l, batch * mem_len, 1024, "xattn_kv")
        o_x = cross_attention(h, wq_b, l, kv, seq, mem_len)
        xf, h = proj_res_norm(o_x, wo_b, l, xf, norm_ffn_g[l], "xattn_out")

        last = l == depth - 1
        g_next = final_norm_g if last else norm_mix_g[l + 1]
        xf, h = ffn(h, w_gate, w_val, conv_w, conv_b, w_down, l, xf, g_next, seq,
                    F32 if last else BF16)
    return h.reshape(batch, seq, d_model)
```

```python
import functools

import numpy as np
import jax
import jax.numpy as jnp
from jax import lax
from jax.experimental import pallas as pl
from jax.experimental.pallas import tpu as pltpu

NORM_EPS = 1e-6

LANES = 128
SUBLANES = 8
BF16_ROWS = 16
VMEM_BUDGET_BYTES = 58 * 1024 * 1024

SWA_Q_HEADS = 16
SWA_KV_HEADS = 4
SWA_HEAD_DIM = 64
SWA_BLOCK = 128
SWA_Q_W = SWA_Q_HEADS * SWA_HEAD_DIM
SWA_KV_W = SWA_KV_HEADS * SWA_HEAD_DIM

RET_HEADS = 8
RET_HEAD_DIM = 128
RET_CHUNK = 128
RET_W = RET_HEADS * RET_HEAD_DIM
RET_ROPE_BASE = 10000.0

LRU_WIDTH = 1024
LRU_BLOCKS = 8
LRU_BLOCK_DIM = LRU_WIDTH // LRU_BLOCKS
LRU_CONV = 4
LRU_C = 8.0

XATTN_HEADS = 4

FFN_CONV = 3
FFN_TILE_N = 512
IN_TILE_N = SWA_Q_W + 2 * SWA_KV_W

LOG2E = 1.4426950408889634

BF16 = jnp.bfloat16
F32 = jnp.float32


def _compiler_params(semantics, vmem_bytes):
    return pltpu.CompilerParams(
        dimension_semantics=semantics,
        vmem_limit_bytes=int(min(max(vmem_bytes, 16 * 1024 * 1024), VMEM_BUDGET_BYTES)))


def _nbytes(shape, dtype):
    return int(np.prod(shape)) * jnp.dtype(dtype).itemsize


def _dot(a, b):
    return jnp.dot(a, b, preferred_element_type=F32)


def _dot_nt(a, b):
    return lax.dot_general(a, b, (((1,), (1,)), ((), ())), preferred_element_type=F32)


def _rms(x, g):
    ms = jnp.mean(x * x, axis=-1, keepdims=True)
    return x * lax.rsqrt(ms + NORM_EPS) * g


def _rmsnorm_kernel(x_ref, g_ref, o_ref):
    o_ref[...] = _rms(x_ref[...], g_ref[...]).astype(o_ref.dtype)


def rmsnorm(x, g, out_dtype=BF16):
    t, d = x.shape
    tm = min(512, t)
    blocks = 2 * (_nbytes((tm, d), F32) + _nbytes((tm, d), out_dtype)) + 2 * _nbytes((tm, d), F32)
    return pl.pallas_call(
        _rmsnorm_kernel,
        grid=(t // tm,),
        in_specs=[pl.BlockSpec((tm, d), lambda i: (i, 0)),
                  pl.BlockSpec((1, d), lambda i: (0, 0))],
        out_specs=pl.BlockSpec((tm, d), lambda i: (i, 0)),
        out_shape=jax.ShapeDtypeStruct((t, d), out_dtype),
        compiler_params=_compiler_params(("parallel",), blocks),
        name="rmsnorm",
    )(x, g.reshape(1, d))


def _matmul_kernel(a_ref, w_ref, o_ref):
    o_ref[...] = _dot(a_ref[...], w_ref[...]).astype(o_ref.dtype)


def matmul(a, w, layer, tm, tn, name):
    m, k = a.shape
    n = w.shape[2]
    tm = min(tm, m)
    blocks = (2 * (_nbytes((tm, k), BF16) + _nbytes((k, tn), BF16) + _nbytes((tm, tn), BF16))
              + 2 * _nbytes((tm, tn), F32))
    return pl.pallas_call(
        _matmul_kernel,
        grid=(m // tm, n // tn),
        in_specs=[pl.BlockSpec((tm, k), lambda i, j: (i, 0)),
                  pl.BlockSpec((None, k, tn), lambda i, j: (layer, 0, j))],
        out_specs=pl.BlockSpec((tm, tn), lambda i, j: (i, j)),
        out_shape=jax.ShapeDtypeStruct((m, n), BF16),
        compiler_params=_compiler_params(("parallel", "arbitrary"), blocks),
        name=name,
    )(a, w)


def _in_proj_kernel(h_ref, wh_ref, w_ref, o_ref):
    j = pl.program_id(1)

    @pl.when(j == 0)
    def _():
        o_ref[...] = _dot(h_ref[...], wh_ref[...]).astype(o_ref.dtype)

    @pl.when(j > 0)
    def _():
        o_ref[...] = _dot(h_ref[...], w_ref[...]).astype(o_ref.dtype)


def in_proj(h, w_head, w_in, layer):
    m, k = h.shape
    n = w_in.shape[2]
    tn = IN_TILE_N
    assert w_head.shape[2] == tn
    tm = min(1024, m)
    nt = n // tn
    blocks = (2 * (_nbytes((tm, k), BF16) + 2 * _nbytes((k, tn), BF16) + _nbytes((tm, tn), BF16))
              + 2 * _nbytes((tm, tn), F32))
    return pl.pallas_call(
        _in_proj_kernel,
        grid=(m // tm, nt),
        in_specs=[pl.BlockSpec((tm, k), lambda i, j: (i, 0)),
                  pl.BlockSpec((None, k, tn), lambda i, j: (layer, 0, 0)),
                  pl.BlockSpec((None, k, tn), lambda i, j: (layer, 0, jnp.maximum(j, 1)))],
        out_specs=pl.BlockSpec((tm, tn), lambda i, j: (i, (j + nt - 1) % nt)),
        out_shape=jax.ShapeDtypeStruct((m, n), BF16),
        compiler_params=_compiler_params(("parallel", "arbitrary"), blocks),
        name="in_proj",
    )(h, w_head, w_in)


def _q_head_order():
    order = []
    g = SWA_Q_HEADS // SWA_KV_HEADS
    for kv_pair in range(SWA_KV_HEADS // 2):
        for i in range(g):
            order += [(2 * kv_pair) * g + i, (2 * kv_pair + 1) * g + i]
    return order


def _swa_kernel(sinks_ref, q_ref, kc_ref, vc_ref, kp_ref, vp_ref, o_ref, *, tq, tiles_per_seq):
    i = pl.program_id(0)
    starts_seq = (i % tiles_per_seq) == 0
    blk = SWA_BLOCK
    kwin = jnp.concatenate([kp_ref[...], kc_ref[...]], axis=0)
    vwin = jnp.concatenate([vp_ref[...], vc_ref[...]], axis=0)
    lane = lax.broadcasted_iota(jnp.int32, (blk, LANES), 1)
    lo = lane < SWA_HEAD_DIM
    row = lax.broadcasted_iota(jnp.int32, (2 * blk, 2 * blk), 0)
    col = lax.broadcasted_iota(jnp.int32, (2 * blk, 2 * blk), 1)
    qi = jnp.bitwise_and(row, blk - 1)
    in_window = jnp.logical_and(col > qi, col <= qi + blk)
    in_window_first = jnp.logical_and(in_window, jnp.logical_or(col >= blk, jnp.logical_not(starts_seq)))
    top = lax.broadcasted_iota(jnp.int32, (2 * blk, 1), 0) < blk
    scale = (SWA_HEAD_DIM ** -0.5) * LOG2E
    order = _q_head_order()
    for c in range(tq // blk):
        allowed = in_window_first if c == 0 else in_window
        for p in range(SWA_Q_W // LANES):
            kv_blk = p // (SWA_Q_W // LANES // (SWA_KV_W // LANES))
            kb = kwin[c * blk:(c + 2) * blk, kv_blk * LANES:(kv_blk + 1) * LANES]
            vb = vwin[c * blk:(c + 2) * blk, kv_blk * LANES:(kv_blk + 1) * LANES]
            qb = q_ref[c * blk:(c + 1) * blk, p * LANES:(p + 1) * LANES]
            zero = jnp.zeros_like(qb)
            qs = jnp.concatenate([jnp.where(lo, qb, zero), jnp.where(lo, zero, qb)], axis=0)
            s = _dot_nt(qs, kb) * scale
            s = jnp.where(allowed, s, -jnp.inf)
            sink = jnp.where(top, sinks_ref[order[2 * p]], sinks_ref[order[2 * p + 1]]) * LOG2E
            m = jnp.maximum(jnp.max(s, axis=-1, keepdims=True), sink)
            e = jnp.exp2(s - m)
            denom = jnp.sum(e, axis=-1, keepdims=True) + jnp.exp2(sink - m)
            probs = (e * (1.0 / denom)).astype(BF16)
            o = _dot(probs, vb)
            ob = jnp.where(lo, o[:blk], o[blk:])
            o_ref[c * blk:(c + 1) * blk, p * LANES:(p + 1) * LANES] = ob.astype(o_ref.dtype)


def swa_attention(proj, sinks, seq, col_q, col_k, col_v):
    t = proj.shape[0]
    tq = min(512, seq)
    nsub = tq // SWA_BLOCK
    qb, kb, vb = col_q // SWA_Q_W, col_k // SWA_KV_W, col_v // SWA_KV_W
    blocks = 2 * (2 * _nbytes((tq, SWA_Q_W), BF16) + 2 * _nbytes((tq + SWA_BLOCK, SWA_KV_W), BF16))
    blocks += 16 * _nbytes((2 * SWA_BLOCK, 2 * SWA_BLOCK), F32)
    kernel = functools.partial(_swa_kernel, tq=tq, tiles_per_seq=seq // tq)
    return pl.pallas_call(
        kernel,
        grid=(t // tq,),
        in_specs=[
            pl.BlockSpec(memory_space=pltpu.SMEM),
            pl.BlockSpec((tq, SWA_Q_W), lambda i: (i, qb)),
            pl.BlockSpec((tq, SWA_KV_W), lambda i: (i, kb)),
            pl.BlockSpec((tq, SWA_KV_W), lambda i: (i, vb)),
            pl.BlockSpec((SWA_BLOCK, SWA_KV_W), lambda i: (jnp.maximum(i * nsub - 1, 0), kb)),
            pl.BlockSpec((SWA_BLOCK, SWA_KV_W), lambda i: (jnp.maximum(i * nsub - 1, 0), vb)),
        ],
        out_specs=pl.BlockSpec((tq, SWA_Q_W), lambda i: (i, 0)),
        out_shape=jax.ShapeDtypeStruct((t, SWA_Q_W), BF16),
        compiler_params=_compiler_params(("parallel",), blocks),
        name="swa_attention",
    )(sinks, proj, proj, proj, proj, proj)


_RET_LOG_GAMMA = [float(v) for v in
                  np.log1p(-np.exp2(-5.0 - np.arange(RET_HEADS, dtype=np.float32))).astype(np.float32)]


def _retention_kernel(q_ref, k_ref, v_ref, g_ref, cos_ref, sin_ref, o_ref, state_ref, decay_ref, *,
                      chunks):
    c = RET_CHUNK
    d = RET_HEAD_DIM

    @pl.when(pl.program_id(1) == 0)
    def _():
        state_ref[...] = jnp.zeros_like(state_ref)
        qpos = lax.broadcasted_iota(jnp.int32, (c, c), 0)
        kpos = lax.broadcasted_iota(jnp.int32, (c, c), 1)
        diff = (qpos - kpos).astype(F32)
        for h in range(RET_HEADS):
            decay_ref[h] = jnp.where(diff >= 0, jnp.exp(jnp.maximum(diff, 0.0) * _RET_LOG_GAMMA[h]), 0.0)

    idx = lax.broadcasted_iota(jnp.int32, (c, 1), 0).astype(F32)
    for h in range(RET_HEADS):
        lg = _RET_LOG_GAMMA[h]
        sl = slice(h * d, (h + 1) * d)
        q_decay = jnp.exp((idx + 1.0) * lg)
        k_decay = jnp.exp((c - 1.0 - idx) * lg)
        chunk_decay = float(np.exp(np.float32(c) * np.float32(lg)))
        state = state_ref[h]
        for n in range(chunks):
            rows = slice(n * c, (n + 1) * c)
            cosf = cos_ref[rows, :]
            sinf = sin_ref[rows, :]
            q = q_ref[rows, sl].astype(F32)
            k = k_ref[rows, sl].astype(F32)
            v = v_ref[rows, sl]
            qr = q * cosf + pltpu.roll(q, d // 2, axis=1) * sinf
            kr = (k * cosf + pltpu.roll(k, d // 2, axis=1) * sinf) * (d ** -0.5)
            scores = _dot_nt(qr.astype(BF16), kr.astype(BF16)) * decay_ref[h]
            intra = _dot(scores.astype(BF16), v)
            inter = _dot((qr * q_decay).astype(BF16), state.astype(BF16))
            kd_t = (kr * k_decay).T.astype(BF16)
            state = state * chunk_decay + _dot(kd_t, v)
            y = intra + inter
            mu = jnp.mean(y, axis=-1, keepdims=True)
            yc = y - mu
            yn = yc * lax.rsqrt(jnp.mean(yc * yc, axis=-1, keepdims=True) + NORM_EPS)
            gate = g_ref[rows, sl].astype(F32)
            o_ref[rows, sl] = (jax.nn.silu(gate) * yn).astype(o_ref.dtype)
        state_ref[h] = state


def _rope_tables(seq):
    half = RET_HEAD_DIM // 2
    theta = RET_ROPE_BASE ** (-jnp.linspace(0.0, 1.0, half, dtype=F32))
    ang = jnp.arange(seq, dtype=F32)[:, None] * theta[None, :]
    cos = jnp.cos(ang)
    sin = jnp.sin(ang)
    return jnp.concatenate([cos, cos], axis=-1), jnp.concatenate([-sin, sin], axis=-1)


def retention(proj, cosf, sinf, batch, seq, col_q):
    t = proj.shape[0]
    c = RET_CHUNK
    tr = min(4 * c, seq)
    nt = seq // tr
    b0 = col_q // RET_W
    blocks = 2 * 5 * _nbytes((tr, RET_W), BF16) + 4 * _nbytes((tr, RET_HEAD_DIM), F32)
    blocks += 2 * _nbytes((RET_HEADS, RET_HEAD_DIM, RET_HEAD_DIM), F32) + 96 * _nbytes((c, c), F32)
    kernel = functools.partial(_retention_kernel, chunks=tr // c)

    def col(j):
        return pl.BlockSpec((tr, RET_W), lambda b, n: (b * nt + n, b0 + j))

    return pl.pallas_call(
        kernel,
        grid=(batch, nt),
        in_specs=[col(0), col(1), col(2), col(3),
                  pl.BlockSpec((tr, RET_HEAD_DIM), lambda b, n: (n, 0)),
                  pl.BlockSpec((tr, RET_HEAD_DIM), lambda b, n: (n, 0))],
        out_specs=pl.BlockSpec((tr, RET_W), lambda b, n: (b * nt + n, 0)),
        out_shape=jax.ShapeDtypeStruct((t, RET_W), BF16),
        scratch_shapes=[pltpu.VMEM((RET_HEADS, RET_HEAD_DIM, RET_HEAD_DIM), F32),
                        pltpu.VMEM((RET_HEADS, c, c), F32)],
        compiler_params=_compiler_params(("arbitrary", "arbitrary"), blocks),
        name="retention",
    )(proj, proj, proj, proj, cosf, sinf)


def _lru_kernel(x_ref, y_ref, cw_ref, cb_ref, wa_ref, ba_ref, wx_ref, bx_ref, lam_ref, o_ref,
                xprev_ref, carry_ref, *, tt):
    w = LRU_WIDTH

    @pl.when(pl.program_id(1) == 0)
    def _():
        xprev_ref[...] = jnp.zeros_like(xprev_ref)
        carry_ref[...] = jnp.zeros_like(carry_ref)

    x = x_ref[...].astype(F32)
    xe = jnp.concatenate([xprev_ref[...], x], axis=0)
    xprev_ref[...] = x[tt - SUBLANES:, :]
    cw = cw_ref[...]
    xc = cb_ref[...] + cw[3:4, :] * x
    for kk in range(LRU_CONV - 1):
        off = SUBLANES - (LRU_CONV - 1) + kk
        xc = xc + cw[kk:kk + 1, :] * xe[off:off + tt, :]

    xcb = xc.astype(BF16)
    gr = []
    gi = []
    for blk in range(LRU_BLOCKS):
        sl = slice(blk * LRU_BLOCK_DIM, (blk + 1) * LRU_BLOCK_DIM)
        gr.append(_dot(xcb[:, sl], wa_ref[blk]))
        gi.append(_dot(xcb[:, sl], wx_ref[blk]))
    gate_r = jax.nn.sigmoid(jnp.concatenate(gr, axis=1) + ba_ref[...])
    gate_i = jax.nn.sigmoid(jnp.concatenate(gi, axis=1) + bx_ref[...])
    z = -lam_ref[...]
    softplus = jnp.maximum(z, 0.0) + jnp.log1p(jnp.exp(-jnp.abs(z)))
    log_a = (-LRU_C) * gate_r * softplus
    a = jnp.exp(log_a)
    u = jnp.sqrt(1.0 - a * a) * (gate_i * xc)

    nv = tt // SUBLANES
    a3 = a.reshape(nv, SUBLANES, w)
    b3 = u.reshape(nv, SUBLANES, w)
    rix = lax.broadcasted_iota(jnp.int32, (nv, SUBLANES, w), 1)
    s = 1
    while s < SUBLANES:
        keep = rix >= s
        a_sh = jnp.where(keep, pltpu.roll(a3, s, axis=1), 1.0)
        b_sh = jnp.where(keep, pltpu.roll(b3, s, axis=1), 0.0)
        b3 = a3 * b_sh + b3
        a3 = a3 * a_sh
        s *= 2
    carry = carry_ref[...]
    hs = []
    for i in range(nv):
        h_i = a3[i] * carry + b3[i]
        carry = h_i[SUBLANES - 1:SUBLANES, :]
        hs.append(h_i)
    carry_ref[...] = carry
    h = jnp.concatenate(hs, axis=0)
    o_ref[...] = (h * jax.nn.gelu(y_ref[...].astype(F32))).astype(o_ref.dtype)


def rg_lru_block(proj, conv_w, conv_b, wa, ba, wx, bx, lam, layer, batch, seq, col_x):
    t = proj.shape[0]
    w = LRU_WIDTH
    tt = min(256, seq)
    nt = seq // tt
    b0 = col_x // w
    blocks = 2 * 3 * _nbytes((tt, w), BF16) + 24 * _nbytes((tt, w), F32)
    blocks += 4 * _nbytes((LRU_BLOCKS, LRU_BLOCK_DIM, LRU_BLOCK_DIM), BF16)
    kernel = functools.partial(_lru_kernel, tt=tt)
    vec = pl.BlockSpec((1, w), lambda b, n: (0, 0))
    blockdiag = pl.BlockSpec((None, LRU_BLOCKS, LRU_BLOCK_DIM, LRU_BLOCK_DIM),
                             lambda b, n: (layer, 0, 0, 0))
    return pl.pallas_call(
        kernel,
        grid=(batch, nt),
        in_specs=[pl.BlockSpec((tt, w), lambda b, n: (b * nt + n, b0)),
                  pl.BlockSpec((tt, w), lambda b, n: (b * nt + n, b0 + 1)),
                  pl.BlockSpec((LRU_CONV, w), lambda b, n: (0, 0)),
                  vec, blockdiag, vec, blockdiag, vec, vec],
        out_specs=pl.BlockSpec((tt, w), lambda b, n: (b * nt + n, 0)),
        out_shape=jax.ShapeDtypeStruct((t, w), BF16),
        scratch_shapes=[pltpu.VMEM((SUBLANES, w), F32), pltpu.VMEM((1, w), F32)],
        compiler_params=_compiler_params(("arbitrary", "arbitrary"), blocks),
        name="rg_lru",
    )(proj, proj, conv_w, conv_b.reshape(1, w), wa, ba.reshape(1, w), wx, bx.reshape(1, w),
      lam.reshape(1, w))


def _merge_kernel(oa_ref, ob_ref, oc_ref, wa_ref, wb_ref, wc_ref, ga_ref, gb_ref, gc_ref, o_ref):
    acc = jax.nn.sigmoid(ga_ref[...].astype(F32)) * _dot(oa_ref[...], wa_ref[...])
    acc = acc + jax.nn.sigmoid(gb_ref[...].astype(F32)) * _dot(ob_ref[...], wb_ref[...])
    acc = acc + jax.nn.sigmoid(gc_ref[...].astype(F32)) * _dot(oc_ref[...], wc_ref[...])
    o_ref[...] = acc.astype(o_ref.dtype)


def branch_merge(o_a, o_b, o_c, w_a_perm, w_branch, layer, proj, col_gates, d_model):
    t, bw = o_a.shape
    tm = min(1024, t)
    tn = 512
    g0 = col_gates // tn
    per_g = d_model // tn
    blocks = 2 * (3 * _nbytes((tm, bw), BF16) + 3 * _nbytes((bw, tn), BF16) + 4 * _nbytes((tm, tn), BF16))
    blocks += 4 * _nbytes((tm, tn), F32)
    o_spec = pl.BlockSpec((tm, bw), lambda i, j: (i, 0))

    def w_spec(idx):
        return pl.BlockSpec((None, None, bw, tn), lambda i, j: (layer, idx, 0, j))

    def gate_spec(idx):
        return pl.BlockSpec((tm, tn), lambda i, j: (i, g0 + idx * per_g + j))

    return pl.pallas_call(
        _merge_kernel,
        grid=(t // tm, d_model // tn),
        in_specs=[o_spec, o_spec, o_spec,
                  pl.BlockSpec((None, bw, tn), lambda i, j: (layer, 0, j)), w_spec(1), w_spec(2),
                  gate_spec(0), gate_spec(1), gate_spec(2)],
        out_specs=pl.BlockSpec((tm, tn), lambda i, j: (i, j)),
        out_shape=jax.ShapeDtypeStruct((t, d_model), BF16),
        compiler_params=_compiler_params(("parallel", "arbitrary"), blocks),
        name="branch_merge",
    )(o_a, o_b, o_c, w_a_perm, w_branch, w_branch, proj, proj, proj)


def _proj_res_norm_kernel(a_ref, w_ref, x_ref, g_ref, xo_ref, ho_ref):
    xn = x_ref[...] + _dot(a_ref[...], w_ref[...])
    xo_ref[...] = xn
    ho_ref[...] = _rms(xn, g_ref[...]).astype(ho_ref.dtype)


def proj_res_norm(a, w, layer, x, g, name):
    t, kdim = a.shape
    d = w.shape[2]
    tm = min(512, t)
    blocks = 2 * (_nbytes((tm, kdim), BF16) + _nbytes((kdim, d), BF16) + 2 * _nbytes((tm, d), F32)
                  + _nbytes((tm, d), BF16)) + 3 * _nbytes((tm, d), F32)
    return pl.pallas_call(
        _proj_res_norm_kernel,
        grid=(t // tm,),
        in_specs=[pl.BlockSpec((tm, kdim), lambda i: (i, 0)),
                  pl.BlockSpec((None, kdim, d), lambda i: (layer, 0, 0)),
                  pl.BlockSpec((tm, d), lambda i: (i, 0)),
                  pl.BlockSpec((1, d), lambda i: (0, 0))],
        out_specs=[pl.BlockSpec((tm, d), lambda i: (i, 0)),
                   pl.BlockSpec((tm, d), lambda i: (i, 0))],
        out_shape=[jax.ShapeDtypeStruct((t, d), F32), jax.ShapeDtypeStruct((t, d), BF16)],
        compiler_params=_compiler_params(("parallel",), blocks),
        name=name,
    )(a, w, x, g.reshape(1, d))


def _xattn_kernel(h_ref, wq_ref, k_ref, v_ref, o_ref, *, head_dim):
    q = _dot(h_ref[...], wq_ref[...]).astype(BF16)
    scale = (head_dim ** -0.5) * LOG2E
    for hd in range(XATTN_HEADS):
        sl = slice(hd * head_dim, (hd + 1) * head_dim)
        s = _dot_nt(q[:, sl], k_ref[:, sl]) * scale
        m = jnp.max(s, axis=-1, keepdims=True)
        e = jnp.exp2(s - m)
        p = (e * (1.0 / jnp.sum(e, axis=-1, keepdims=True))).astype(BF16)
        o_ref[:, sl] = _dot(p, v_ref[:, sl]).astype(o_ref.dtype)


def cross_attention(h, wq, layer, kv, seq, mem_len):
    t, d = h.shape
    tm = min(512, seq)
    tiles_per_seq = seq // tm
    blocks = 2 * (2 * _nbytes((tm, d), BF16) + _nbytes((d, d), BF16) + 2 * _nbytes((mem_len, d), BF16))
    blocks += 3 * _nbytes((tm, d), F32)
    kernel = functools.partial(_xattn_kernel, head_dim=d // XATTN_HEADS)
    return pl.pallas_call(
        kernel,
        grid=(t // tm,),
        in_specs=[pl.BlockSpec((tm, d), lambda i: (i, 0)),
                  pl.BlockSpec((None, d, d), lambda i: (layer, 0, 0)),
                  pl.BlockSpec((mem_len, d), lambda i: (i // tiles_per_seq, 0)),
                  pl.BlockSpec((mem_len, d), lambda i: (i // tiles_per_seq, 1))],
        out_specs=pl.BlockSpec((tm, d), lambda i: (i, 0)),
        out_shape=jax.ShapeDtypeStruct((t, d), BF16),
        compiler_params=_compiler_params(("parallel",), blocks),
        name="cross_attention",
    )(h, wq, kv, kv)


def _ffn_kernel(h_ref, halo_ref, wg_ref, wv_ref, cw_ref, cb_ref, wd_ref, x_ref, g_ref, xo_ref, ho_ref,
                hext_ref, a0_ref, a1_ref, acc_ref, *, tm, n_tiles, tiles_per_seq):
    i = pl.program_id(0)
    j = pl.program_id(1)
    halo = BF16_ROWS
    acts = (a0_ref, a1_ref)

    def up_gate(slot):
        g = _dot(hext_ref[...], wg_ref[...])
        v = _dot(h_ref[...], wv_ref[...])
        return g, v

    def gate(g, v, slot):
        cw = cw_ref[...]
        conv = cb_ref[...] + cw[FFN_CONV - 1:FFN_CONV, :] * g[halo:, :]
        for kk in range(FFN_CONV - 1):
            off = halo - (FFN_CONV - 1) + kk
            conv = conv + cw[kk:kk + 1, :] * g[off:off + tm, :]
        acts[slot][...] = (jax.nn.silu(conv) * v).astype(BF16)

    def down(slot):
        acc_ref[...] += _dot(acts[slot][...], wd_ref[...])

    @pl.when(j == 0)
    def _():
        keep = (i % tiles_per_seq) != 0
        hext_ref[0:halo, :] = jnp.where(keep, halo_ref[...], jnp.zeros_like(halo_ref))
        hext_ref[halo:, :] = h_ref[...]
        acc_ref[...] = jnp.zeros_like(acc_ref)
        g, v = up_gate(0)
        gate(g, v, 0)

    mid = jnp.logical_and(j > 0, j < n_tiles)

    @pl.when(jnp.logical_and(mid, j % 2 == 1))
    def _():
        g, v = up_gate(1)
        down(0)
        gate(g, v, 1)

    @pl.when(jnp.logical_and(mid, j % 2 == 0))
    def _():
        g, v = up_gate(0)
        down(1)
        gate(g, v, 0)

    @pl.when(j == n_tiles)
    def _():
        down((n_tiles - 1) % 2)
        xn = x_ref[...] + acc_ref[...]
        xo_ref[...] = xn
        ho_ref[...] = _rms(xn, g_ref[...]).astype(ho_ref.dtype)


def ffn(h, w_gate, w_val, conv_w, conv_b, w_down, layer, x, g, seq, h_dtype):
    t, d = h.shape
    n = w_gate.shape[2]
    tn = FFN_TILE_N
    tm = min(512, seq)
    n_tiles = n // tn
    halo = BF16_ROWS
    blocks = 2 * (_nbytes((tm, d), BF16) + 2 * _nbytes((d, tn), BF16) + _nbytes((tn, d), BF16)
                  + 2 * _nbytes((tm, d), F32) + _nbytes((tm, d), h_dtype))
    blocks += _nbytes((tm + halo, d), BF16) + _nbytes((tm, d), F32)
    blocks += 6 * _nbytes((tm, tn), F32) + 2 * _nbytes((tm, d), F32)
    kernel = functools.partial(_ffn_kernel, tm=tm, n_tiles=n_tiles, tiles_per_seq=seq // tm)

    def up_tile(j):
        return jnp.minimum(j, n_tiles - 1)

    def down_tile(j):
        return jnp.maximum(j - 1, 0)

    return pl.pallas_call(
        kernel,
        grid=(t // tm, n_tiles + 1),
        in_specs=[pl.BlockSpec((tm, d), lambda i, j: (i, 0)),
                  pl.BlockSpec((halo, d), lambda i, j: (jnp.maximum(i * (tm // halo) - 1, 0), 0)),
                  pl.BlockSpec((None, d, tn), lambda i, j: (layer, 0, up_tile(j))),
                  pl.BlockSpec((None, d, tn), lambda i, j: (layer, 0, up_tile(j))),
                  pl.BlockSpec((None, FFN_CONV, tn), lambda i, j: (layer, 0, up_tile(j))),
                  pl.BlockSpec((None, 1, tn), lambda i, j: (layer, 0, up_tile(j))),
                  pl.BlockSpec((None, tn, d), lambda i, j: (layer, down_tile(j), 0)),
                  pl.BlockSpec((tm, d), lambda i, j: (i, 0)),
                  pl.BlockSpec((1, d), lambda i, j: (0, 0))],
        out_specs=[pl.BlockSpec((tm, d), lambda i, j: (i, 0)),
                   pl.BlockSpec((tm, d), lambda i, j: (i, 0))],
        out_shape=[jax.ShapeDtypeStruct((t, d), F32), jax.ShapeDtypeStruct((t, d), h_dtype)],
        scratch_shapes=[pltpu.VMEM((tm + halo, d), BF16),
                        pltpu.VMEM((tm, tn), BF16), pltpu.VMEM((tm, tn), BF16),
                        pltpu.VMEM((tm, d), F32)],
        compiler_params=_compiler_params(("parallel", "arbitrary"), blocks),
        name="ffn",
    )(h, h, w_gate, w_val, conv_w, conv_b, w_down, x, g.reshape(1, d))


def _pad_last(a, n):
    return jnp.pad(a, [(0, 0)] * (a.ndim - 1) + [(0, n - a.shape[-1])])


def kernel(x, mem, mem_norm_g, norm_mix_g, w_in, attn_sinks, lru_conv_w, lru_conv_b, lru_wa, lru_ba,
           lru_wx, lru_bx, lru_lambda, w_branch, w_out, norm_xattn_g, xattn_wq, xattn_wkv, xattn_wo,
           norm_ffn_g, ffn_w_up, ffn_conv_w, ffn_conv_b, ffn_w_down, final_norm_g):
    batch, seq, d_model = x.shape
    depth = w_in.shape[0]
    mem_len = mem.shape[1]
    ffn_dim = ffn_w_down.shape[1]
    t = batch * seq
    assert seq % 512 == 0 or seq in (128, 256)
    assert d_model == 2048 and w_in.shape[2] == 13824

    col_ret, col_lru, col_gate = 0, 4 * RET_W, 4 * RET_W + 2 * LRU_WIDTH
    col_q = col_gate + 3 * d_model
    col_k = col_q + SWA_Q_W
    col_v = col_k + SWA_KV_W

    perm_heads = np.asarray(_q_head_order())
    q_cols = (perm_heads[:, None] * SWA_HEAD_DIM + np.arange(SWA_HEAD_DIM)[None, :]).reshape(-1)
    w_in_b = w_in.astype(BF16)
    w_head = jnp.concatenate([w_in[:, :, :SWA_Q_W][:, :, q_cols], w_in[:, :, SWA_Q_W:IN_TILE_N]],
                             axis=2).astype(BF16)
    w_branch_b = w_branch.astype(BF16)
    w_branch_a = w_branch[:, 0][:, q_cols, :].astype(BF16)
    w_out_b = w_out.astype(BF16)
    wq_b = xattn_wq.astype(BF16)
    wkv_b = xattn_wkv.astype(BF16)
    wo_b = xattn_wo.astype(BF16)
    lru_wa_b = lru_wa.astype(BF16)
    lru_wx_b = lru_wx.astype(BF16)
    ffn_pad = -(-ffn_dim // FFN_TILE_N) * FFN_TILE_N
    w_gate = _pad_last(ffn_w_up[:, :, :ffn_dim], ffn_pad).astype(BF16)
    w_val = _pad_last(ffn_w_up[:, :, ffn_dim:], ffn_pad).astype(BF16)
    w_down = jnp.pad(ffn_w_down, ((0, 0), (0, ffn_pad - ffn_dim), (0, 0))).astype(BF16)
    conv_w = _pad_last(ffn_conv_w, ffn_pad)
    conv_b = _pad_last(ffn_conv_b, ffn_pad)[:, None, :]
    cosf, sinf = _rope_tables(seq)

    xf = x.reshape(t, d_model)
    mem_n = rmsnorm(mem.reshape(batch * mem_len, d_model), mem_norm_g)
    h = rmsnorm(xf, norm_mix_g[0])
    for l in range(depth):
        proj = in_proj(h, w_head, w_in_b, l)
        o_a = swa_attention(proj, attn_sinks[l], seq, col_q, col_k, col_v)
        o_b = retention(proj, cosf, sinf, batch, seq, col_ret)
        o_c = rg_lru_block(proj, lru_conv_w[l], lru_conv_b[l], lru_wa_b, lru_ba[l], lru_wx_b,
                           lru_bx[l], lru_lambda[l], l, batch, seq, col_lru)
        merged = branch_merge(o_a, o_b, o_c, w_branch_a, w_branch_b, l, proj, col_gate, d_model)
        xf, h = proj_res_norm(merged, w_out_b, l, xf, norm_xattn_g[l], "mixer_out")

        kv = matmul(mem_n, wkv_b, l, batch * mem_len, 1024, "xattn_kv")
        o_x = cross_attention(h, wq_b, l, kv, seq, mem_len)
        xf, h = proj_res_norm(o_x, wo_b, l, xf, norm_ffn_g[l], "xattn_out")

        last = l == depth - 1
        g_next = final_norm_g if last else norm_mix_g[l + 1]
        xf, h = ffn(h, w_gate, w_val, conv_w, conv_b, w_down, l, xf, g_next, seq,
                    F32 if last else BF16)
    return h.reshape(batch, seq, d_model)
```

```python
import functools

import numpy as np
import jax
import jax.numpy as jnp
from jax import lax
from jax.experimental import pallas as pl
from jax.experimental.pallas import tpu as pltpu

NORM_EPS = 1e-6

LANES = 128
SUBLANES = 8
BF16_ROWS = 16
VMEM_BUDGET_BYTES = 58 * 1024 * 1024

SWA_Q_HEADS = 16
SWA_KV_HEADS = 4
SWA_HEAD_DIM = 64
SWA_BLOCK = 128
SWA_Q_W = SWA_Q_HEADS * SWA_HEAD_DIM
SWA_KV_W = SWA_KV_HEADS * SWA_HEAD_DIM

RET_HEADS = 8
RET_HEAD_DIM = 128
RET_CHUNK = 128
RET_W = RET_HEADS * RET_HEAD_DIM
RET_ROPE_BASE = 10000.0

LRU_WIDTH = 1024
LRU_BLOCKS = 8
LRU_BLOCK_DIM = LRU_WIDTH // LRU_BLOCKS
LRU_CONV = 4
LRU_C = 8.0

XATTN_HEADS = 4

FFN_CONV = 3
FFN_TILE_N = 512
IN_TILE_N = SWA_Q_W + 2 * SWA_KV_W

LOG2E = 1.4426950408889634

BF16 = jnp.bfloat16
F32 = jnp.float32


def _compiler_params(semantics, vmem_bytes):
    return pltpu.CompilerParams(
        dimension_semantics=semantics,
        vmem_limit_bytes=int(min(max(vmem_bytes, 16 * 1024 * 1024), VMEM_BUDGET_BYTES)))


def _nbytes(shape, dtype):
    return int(np.prod(shape)) * jnp.dtype(dtype).itemsize


def _dot(a, b):
    return jnp.dot(a, b, preferred_element_type=F32)


def _dot_nt(a, b):
    return lax.dot_general(a, b, (((1,), (1,)), ((), ())), preferred_element_type=F32)


def _rms(x, g):
    ms = jnp.mean(x * x, axis=-1, keepdims=True)
    return x * lax.rsqrt(ms + NORM_EPS) * g


def _rmsnorm_kernel(x_ref, g_ref, o_ref):
    o_ref[...] = _rms(x_ref[...], g_ref[...]).astype(o_ref.dtype)


def rmsnorm(x, g, out_dtype=BF16):
    t, d = x.shape
    tm = min(512, t)
    blocks = 2 * (_nbytes((tm, d), F32) + _nbytes((tm, d), out_dtype)) + 2 * _nbytes((tm, d), F32)
    return pl.pallas_call(
        _rmsnorm_kernel,
        grid=(t // tm,),
        in_specs=[pl.BlockSpec((tm, d), lambda i: (i, 0)),
                  pl.BlockSpec((1, d), lambda i: (0, 0))],
        out_specs=pl.BlockSpec((tm, d), lambda i: (i, 0)),
        out_shape=jax.ShapeDtypeStruct((t, d), out_dtype),
        compiler_params=_compiler_params(("parallel",), blocks),
        name="rmsnorm",
    )(x, g.reshape(1, d))


def _matmul_kernel(a_ref, w_ref, o_ref):
    o_ref[...] = _dot(a_ref[...], w_ref[...]).astype(o_ref.dtype)


def matmul(a, w, layer, tm, tn, name):
    m, k = a.shape
    n = w.shape[2]
    tm = min(tm, m)
    blocks = (2 * (_nbytes((tm, k), BF16) + _nbytes((k, tn), BF16) + _nbytes((tm, tn), BF16))
              + 2 * _nbytes((tm, tn), F32))
    return pl.pallas_call(
        _matmul_kernel,
        grid=(m // tm, n // tn),
        in_specs=[pl.BlockSpec((tm, k), lambda i, j: (i, 0)),
                  pl.BlockSpec((None, k, tn), lambda i, j: (layer, 0, j))],
        out_specs=pl.BlockSpec((tm, tn), lambda i, j: (i, j)),
        out_shape=jax.ShapeDtypeStruct((m, n), BF16),
        compiler_params=_compiler_params(("parallel", "arbitrary"), blocks),
        name=name,
    )(a, w)


def _in_proj_kernel(h_ref, wh_ref, w_ref, o_ref):
    j = pl.program_id(1)

    @pl.when(j == 0)
    def _():
        o_ref[...] = _dot(h_ref[...], wh_ref[...]).astype(o_ref.dtype)

    @pl.when(j > 0)
    def _():
        o_ref[...] = _dot(h_ref[...], w_ref[...]).astype(o_ref.dtype)


def in_proj(h, w_head, w_in, layer):
    m, k = h.shape
    n = w_in.shape[2]
    tn = IN_TILE_N
    assert w_head.shape[2] == tn
    tm = min(1024, m)
    nt = n // tn
    blocks = (2 * (_nbytes((tm, k), BF16) + 2 * _nbytes((k, tn), BF16) + _nbytes((tm, tn), BF16))
              + 2 * _nbytes((tm, tn), F32))
    return pl.pallas_call(
        _in_proj_kernel,
        grid=(m // tm, nt),
        in_specs=[pl.BlockSpec((tm, k), lambda i, j: (i, 0)),
                  pl.BlockSpec((None, k, tn), lambda i, j: (layer, 0, 0)),
                  pl.BlockSpec((None, k, tn), lambda i, j: (layer, 0, jnp.maximum(j, 1)))],
        out_specs=pl.BlockSpec((tm, tn), lambda i, j: (i, (j + nt - 1) % nt)),
        out_shape=jax.ShapeDtypeStruct((m, n), BF16),
        compiler_params=_compiler_params(("parallel", "arbitrary"), blocks),
        name="in_proj",
    )(h, w_head, w_in)


def _q_head_order():
    order = []
    g = SWA_Q_HEADS // SWA_KV_HEADS
    for kv_pair in range(SWA_KV_HEADS // 2):
        for i in range(g):
            order += [(2 * kv_pair) * g + i, (2 * kv_pair + 1) * g + i]
    return order


def _swa_kernel(sinks_ref, q_ref, kc_ref, vc_ref, kp_ref, vp_ref, o_ref, *, tq, tiles_per_seq):
    i = pl.program_id(0)
    starts_seq = (i % tiles_per_seq) == 0
    blk = SWA_BLOCK
    kwin = jnp.concatenate([kp_ref[...], kc_ref[...]], axis=0)
    vwin = jnp.concatenate([vp_ref[...], vc_ref[...]], axis=0)
    lane = lax.broadcasted_iota(jnp.int32, (blk, LANES), 1)
    lo = lane < SWA_HEAD_DIM
    row = lax.broadcasted_iota(jnp.int32, (2 * blk, 2 * blk), 0)
    col = lax.broadcasted_iota(jnp.int32, (2 * blk, 2 * blk), 1)
    qi = jnp.bitwise_and(row, blk - 1)
    in_window = jnp.logical_and(col > qi, col <= qi + blk)
    in_window_first = jnp.logical_and(in_window, jnp.logical_or(col >= blk, jnp.logical_not(starts_seq)))
    top = lax.broadcasted_iota(jnp.int32, (2 * blk, 1), 0) < blk
    scale = (SWA_HEAD_DIM ** -0.5) * LOG2E
    order = _q_head_order()
    for c in range(tq // blk):
        allowed = in_window_first if c == 0 else in_window
        for p in range(SWA_Q_W // LANES):
            kv_blk = p // (SWA_Q_W // LANES // (SWA_KV_W // LANES))
            kb = kwin[c * blk:(c + 2) * blk, kv_blk * LANES:(kv_blk + 1) * LANES]
            vb = vwin[c * blk:(c + 2) * blk, kv_blk * LANES:(kv_blk + 1) * LANES]
            qb = q_ref[c * blk:(c + 1) * blk, p * LANES:(p + 1) * LANES]
            zero = jnp.zeros_like(qb)
            qs = jnp.concatenate([jnp.where(lo, qb, zero), jnp.where(lo, zero, qb)], axis=0)
            s = _dot_nt(qs, kb) * scale
            s = jnp.where(allowed, s, -jnp.inf)
            sink = jnp.where(top, sinks_ref[order[2 * p]], sinks_ref[order[2 * p + 1]]) * LOG2E
            m = jnp.maximum(jnp.max(s, axis=-1, keepdims=True), sink)
            e = jnp.exp2(s - m)
            denom = jnp.sum(e, axis=-1, keepdims=True) + jnp.exp2(sink - m)
            probs = (e * (1.0 / denom)).astype(BF16)
            o = _dot(probs, vb)
            ob = jnp.where(lo, o[:blk], o[blk:])
            o_ref[c * blk:(c + 1) * blk, p * LANES:(p + 1) * LANES] = ob.astype(o_ref.dtype)


def swa_attention(proj, sinks, seq, col_q, col_k, col_v):
    t = proj.shape[0]
    tq = min(512, seq)
    nsub = tq // SWA_BLOCK
    qb, kb, vb = col_q // SWA_Q_W, col_k // SWA_KV_W, col_v // SWA_KV_W
    blocks = 2 * (2 * _nbytes((tq, SWA_Q_W), BF16) + 2 * _nbytes((tq + SWA_BLOCK, SWA_KV_W), BF16))
    blocks += 16 * _nbytes((2 * SWA_BLOCK, 2 * SWA_BLOCK), F32)
    kernel = functools.partial(_swa_kernel, tq=tq, tiles_per_seq=seq // tq)
    return pl.pallas_call(
        kernel,
        grid=(t // tq,),
        in_specs=[
            pl.BlockSpec(memory_space=pltpu.SMEM),
            pl.BlockSpec((tq, SWA_Q_W), lambda i: (i, qb)),
            pl.BlockSpec((tq, SWA_KV_W), lambda i: (i, kb)),
            pl.BlockSpec((tq, SWA_KV_W), lambda i: (i, vb)),
            pl.BlockSpec((SWA_BLOCK, SWA_KV_W), lambda i: (jnp.maximum(i * nsub - 1, 0), kb)),
            pl.BlockSpec((SWA_BLOCK, SWA_KV_W), lambda i: (jnp.maximum(i * nsub - 1, 0), vb)),
        ],
        out_specs=pl.BlockSpec((tq, SWA_Q_W), lambda i: (i, 0)),
        out_shape=jax.ShapeDtypeStruct((t, SWA_Q_W), BF16),
        compiler_params=_compiler_params(("parallel",), blocks),
        name="swa_attention",
    )(sinks, proj, proj, proj, proj, proj)


_RET_LOG_GAMMA = [float(v) for v in
                  np.log1p(-np.exp2(-5.0 - np.arange(RET_HEADS, dtype=np.float32))).astype(np.float32)]


def _retention_kernel(q_ref, k_ref, v_ref, g_ref, cos_ref, sin_ref, o_ref, state_ref, decay_ref, *,
                      chunks):
    c = RET_CHUNK
    d = RET_HEAD_DIM

    @pl.when(pl.program_id(1) == 0)
    def _():
        state_ref[...] = jnp.zeros_like(state_ref)
        qpos = lax.broadcasted_iota(jnp.int32, (c, c), 0)
        kpos = lax.broadcasted_iota(jnp.int32, (c, c), 1)
        diff = (qpos - kpos).astype(F32)
        for h in range(RET_HEADS):
            decay_ref[h] = jnp.where(diff >= 0, jnp.exp(jnp.maximum(diff, 0.0) * _RET_LOG_GAMMA[h]), 0.0)

    idx = lax.broadcasted_iota(jnp.int32, (c, 1), 0).astype(F32)
    for h in range(RET_HEADS):
        lg = _RET_LOG_GAMMA[h]
        sl = slice(h * d, (h + 1) * d)
        q_decay = jnp.exp((idx + 1.0) * lg)
        k_decay = jnp.exp((c - 1.0 - idx) * lg)
        chunk_decay = float(np.exp(np.float32(c) * np.float32(lg)))
        state = state_ref[h]
        for n in range(chunks):
            rows = slice(n * c, (n + 1) * c)
            cosf = cos_ref[rows, :]
            sinf = sin_ref[rows, :]
            q = q_ref[rows, sl].astype(F32)
            k = k_ref[rows, sl].astype(F32)
            v = v_ref[rows, sl]
            qr = q * cosf + pltpu.roll(q, d // 2, axis=1) * sinf
            kr = (k * cosf + pltpu.roll(k, d // 2, axis=1) * sinf) * (d ** -0.5)
            scores = _dot_nt(qr.astype(BF16), kr.astype(BF16)) * decay_ref[h]
            intra = _dot(scores.astype(BF16), v)
            inter = _dot((qr * q_decay).astype(BF16), state.astype(BF16))
            kd_t = (kr * k_decay).T.astype(BF16)
            state = state * chunk_decay + _dot(kd_t, v)
            y = intra + inter
            mu = jnp.mean(y, axis=-1, keepdims=True)
            yc = y - mu
            yn = yc * lax.rsqrt(jnp.mean(yc * yc, axis=-1, keepdims=True) + NORM_EPS)
            gate = g_ref[rows, sl].astype(F32)
            o_ref[rows, sl] = (jax.nn.silu(gate) * yn).astype(o_ref.dtype)
        state_ref[h] = state


def _rope_tables(seq):
    half = RET_HEAD_DIM // 2
    theta = RET_ROPE_BASE ** (-jnp.linspace(0.0, 1.0, half, dtype=F32))
    ang = jnp.arange(seq, dtype=F32)[:, None] * theta[None, :]
    cos = jnp.cos(ang)
    sin = jnp.sin(ang)
    return jnp.concatenate([cos, cos], axis=-1), jnp.concatenate([-sin, sin], axis=-1)


def retention(proj, cosf, sinf, batch, seq, col_q):
    t = proj.shape[0]
    c = RET_CHUNK
    tr = min(4 * c, seq)
    nt = seq // tr
    b0 = col_q // RET_W
    blocks = 2 * 5 * _nbytes((tr, RET_W), BF16) + 4 * _nbytes((tr, RET_HEAD_DIM), F32)
    blocks += 2 * _nbytes((RET_HEADS, RET_HEAD_DIM, RET_HEAD_DIM), F32) + 96 * _nbytes((c, c), F32)
    kernel = functools.partial(_retention_kernel, chunks=tr // c)

    def col(j):
        return pl.BlockSpec((tr, RET_W), lambda b, n: (b * nt + n, b0 + j))

    return pl.pallas_call(
        kernel,
        grid=(batch, nt),
        in_specs=[col(0), col(1), col(2), col(3),
                  pl.BlockSpec((tr, RET_HEAD_DIM), lambda b, n: (n, 0)),
                  pl.BlockSpec((tr, RET_HEAD_DIM), lambda b, n: (n, 0))],
        out_specs=pl.BlockSpec((tr, RET_W), lambda b, n: (b * nt + n, 0)),
        out_shape=jax.ShapeDtypeStruct((t, RET_W), BF16),
        scratch_shapes=[pltpu.VMEM((RET_HEADS, RET_HEAD_DIM, RET_HEAD_DIM), F32),
                        pltpu.VMEM((RET_HEADS, c, c), F32)],
        compiler_params=_compiler_params(("arbitrary", "arbitrary"), blocks),
        name="retention",
    )(proj, proj, proj, proj, cosf, sinf)


def _lru_kernel(x_ref, y_ref, cw_ref, cb_ref, wa_ref, ba_ref, wx_ref, bx_ref, lam_ref, o_ref,
                xprev_ref, carry_ref, *, tt):
    w = LRU_WIDTH

    @pl.when(pl.program_id(1) == 0)
    def _():
        xprev_ref[...] = jnp.zeros_like(xprev_ref)
        carry_ref[...] = jnp.zeros_like(carry_ref)

    x = x_ref[...].astype(F32)
    xe = jnp.concatenate([xprev_ref[...], x], axis=0)
    xprev_ref[...] = x[tt - SUBLANES:, :]
    cw = cw_ref[...]
    xc = cb_ref[...] + cw[3:4, :] * x
    for kk in range(LRU_CONV - 1):
        off = SUBLANES - (LRU_CONV - 1) + kk
        xc = xc + cw[kk:kk + 1, :] * xe[off:off + tt, :]

    xcb = xc.astype(BF16)
    gr = []
    gi = []
    for blk in range(LRU_BLOCKS):
        sl = slice(blk * LRU_BLOCK_DIM, (blk + 1) * LRU_BLOCK_DIM)
        gr.append(_dot(xcb[:, sl], wa_ref[blk]))
        gi.append(_dot(xcb[:, sl], wx_ref[blk]))
    gate_r = jax.nn.sigmoid(jnp.concatenate(gr, axis=1) + ba_ref[...])
    gate_i = jax.nn.sigmoid(jnp.concatenate(gi, axis=1) + bx_ref[...])
    z = -lam_ref[...]
    softplus = jnp.maximum(z, 0.0) + jnp.log1p(jnp.exp(-jnp.abs(z)))
    log_a = (-LRU_C) * gate_r * softplus
    a = jnp.exp(log_a)
    u = jnp.sqrt(1.0 - a * a) * (gate_i * xc)

    nv = tt // SUBLANES
    a3 = a.reshape(nv, SUBLANES, w)
    b3 = u.reshape(nv, SUBLANES, w)
    rix = lax.broadcasted_iota(jnp.int32, (nv, SUBLANES, w), 1)
    s = 1
    while s < SUBLANES:
        keep = rix >= s
        a_sh = jnp.where(keep, pltpu.roll(a3, s, axis=1), 1.0)
        b_sh = jnp.where(keep, pltpu.roll(b3, s, axis=1), 0.0)
        b3 = a3 * b_sh + b3
        a3 = a3 * a_sh
        s *= 2
    carry = carry_ref[...]
    hs = []
    for i in range(nv):
        h_i = a3[i] * carry + b3[i]
        carry = h_i[SUBLANES - 1:SUBLANES, :]
        hs.append(h_i)
    carry_ref[...] = carry
    h = jnp.concatenate(hs, axis=0)
    o_ref[...] = (h * jax.nn.gelu(y_ref[...].astype(F32))).astype(o_ref.dtype)


def rg_lru_block(proj, conv_w, conv_b, wa, ba, wx, bx, lam, layer, batch, seq, col_x):
    t = proj.shape[0]
    w = LRU_WIDTH
    tt = min(256, seq)
    nt = seq // tt
    b0 = col_x // w
    blocks = 2 * 3 * _nbytes((tt, w), BF16) + 24 * _nbytes((tt, w), F32)
    blocks += 4 * _nbytes((LRU_BLOCKS, LRU_BLOCK_DIM, LRU_BLOCK_DIM), BF16)
    kernel = functools.partial(_lru_kernel, tt=tt)
    vec = pl.BlockSpec((1, w), lambda b, n: (0, 0))
    blockdiag = pl.BlockSpec((None, LRU_BLOCKS, LRU_BLOCK_DIM, LRU_BLOCK_DIM),
                             lambda b, n: (layer, 0, 0, 0))
    return pl.pallas_call(
        kernel,
        grid=(batch, nt),
        in_specs=[pl.BlockSpec((tt, w), lambda b, n: (b * nt + n, b0)),
                  pl.BlockSpec((tt, w), lambda b, n: (b * nt + n, b0 + 1)),
                  pl.BlockSpec((LRU_CONV, w), lambda b, n: (0, 0)),
                  vec, blockdiag, vec, blockdiag, vec, vec],
        out_specs=pl.BlockSpec((tt, w), lambda b, n: (b * nt + n, 0)),
        out_shape=jax.ShapeDtypeStruct((t, w), BF16),
        scratch_shapes=[pltpu.VMEM((SUBLANES, w), F32), pltpu.VMEM((1, w), F32)],
        compiler_params=_compiler_params(("arbitrary", "arbitrary"), blocks),
        name="rg_lru",
    )(proj, proj, conv_w, conv_b.reshape(1, w), wa, ba.reshape(1, w), wx, bx.reshape(1, w),
      lam.reshape(1, w))


def _merge_kernel(oa_ref, ob_ref, oc_ref, wa_ref, wb_ref, wc_ref, ga_ref, gb_ref, gc_ref, o_ref):
    acc = jax.nn.sigmoid(ga_ref[...].astype(F32)) * _dot(oa_ref[...], wa_ref[...])
    acc = acc + jax.nn.sigmoid(gb_ref[...].astype(F32)) * _dot(ob_ref[...], wb_ref[...])
    acc = acc + jax.nn.sigmoid(gc_ref[...].astype(F32)) * _dot(oc_ref[...], wc_ref[...])
    o_ref[...] = acc.astype(o_ref.dtype)


def branch_merge(o_a, o_b, o_c, w_a_perm, w_branch, layer, proj, col_gates, d_model):
    t, bw = o_a.shape
    tm = min(1024, t)
    tn = 512
    g0 = col_gates // tn
    per_g = d_model // tn
    blocks = 2 * (3 * _nbytes((tm, bw), BF16) + 3 * _nbytes((bw, tn), BF16) + 4 * _nbytes((tm, tn), BF16))
    blocks += 4 * _nbytes((tm, tn), F32)
    o_spec = pl.BlockSpec((tm, bw), lambda i, j: (i, 0))

    def w_spec(idx):
        return pl.BlockSpec((None, None, bw, tn), lambda i, j: (layer, idx, 0, j))

    def gate_spec(idx):
        return pl.BlockSpec((tm, tn), lambda i, j: (i, g0 + idx * per_g + j))

    return pl.pallas_call(
        _merge_kernel,
        grid=(t // tm, d_model // tn),
        in_specs=[o_spec, o_spec, o_spec,
                  pl.BlockSpec((None, bw, tn), lambda i, j: (layer, 0, j)), w_spec(1), w_spec(2),
                  gate_spec(0), gate_spec(1), gate_spec(2)],
        out_specs=pl.BlockSpec((tm, tn), lambda i, j: (i, j)),
        out_shape=jax.ShapeDtypeStruct((t, d_model), BF16),
        compiler_params=_compiler_params(("parallel", "arbitrary"), blocks),
        name="branch_merge",
    )(o_a, o_b, o_c, w_a_perm, w_branch, w_branch, proj, proj, proj)


def _proj_res_norm_kernel(a_ref, w_ref, x_ref, g_ref, xo_ref, ho_ref):
    xn = x_ref[...] + _dot(a_ref[...], w_ref[...])
    xo_ref[...] = xn
    ho_ref[...] = _rms(xn, g_ref[...]).astype(ho_ref.dtype)


def proj_res_norm(a, w, layer, x, g, name):
    t, kdim = a.shape
    d = w.shape[2]
    tm = min(512, t)
    blocks = 2 * (_nbytes((tm, kdim), BF16) + _nbytes((kdim, d), BF16) + 2 * _nbytes((tm, d), F32)
                  + _nbytes((tm, d), BF16)) + 3 * _nbytes((tm, d), F32)
    return pl.pallas_call(
        _proj_res_norm_kernel,
        grid=(t // tm,),
        in_specs=[pl.BlockSpec((tm, kdim), lambda i: (i, 0)),
                  pl.BlockSpec((None, kdim, d), lambda i: (layer, 0, 0)),
                  pl.BlockSpec((tm, d), lambda i: (i, 0)),
                  pl.BlockSpec((1, d), lambda i: (0, 0))],
        out_specs=[pl.BlockSpec((tm, d), lambda i: (i, 0)),
                   pl.BlockSpec((tm, d), lambda i: (i, 0))],
        out_shape=[jax.ShapeDtypeStruct((t, d), F32), jax.ShapeDtypeStruct((t, d), BF16)],
        compiler_params=_compiler_params(("parallel",), blocks),
        name=name,
    )(a, w, x, g.reshape(1, d))


def _xattn_kernel(h_ref, wq_ref, k_ref, v_ref, o_ref, *, head_dim):
    q = _dot(h_ref[...], wq_ref[...]).astype(BF16)
    scale = (head_dim ** -0.5) * LOG2E
    for hd in range(XATTN_HEADS):
        sl = slice(hd * head_dim, (hd + 1) * head_dim)
        s = _dot_nt(q[:, sl], k_ref[:, sl]) * scale
        m = jnp.max(s, axis=-1, keepdims=True)
        e = jnp.exp2(s - m)
        p = (e * (1.0 / jnp.sum(e, axis=-1, keepdims=True))).astype(BF16)
        o_ref[:, sl] = _dot(p, v_ref[:, sl]).astype(o_ref.dtype)


def cross_attention(h, wq, layer, kv, seq, mem_len):
    t, d = h.shape
    tm = min(512, seq)
    tiles_per_seq = seq // tm
    blocks = 2 * (2 * _nbytes((tm, d), BF16) + _nbytes((d, d), BF16) + 2 * _nbytes((mem_len, d), BF16))
    blocks += 3 * _nbytes((tm, d), F32)
    kernel = functools.partial(_xattn_kernel, head_dim=d // XATTN_HEADS)
    return pl.pallas_call(
        kernel,
        grid=(t // tm,),
        in_specs=[pl.BlockSpec((tm, d), lambda i: (i, 0)),
                  pl.BlockSpec((None, d, d), lambda i: (layer, 0, 0)),
                  pl.BlockSpec((mem_len, d), lambda i: (i // tiles_per_seq, 0)),
                  pl.BlockSpec((mem_len, d), lambda i: (i // tiles_per_seq, 1))],
        out_specs=pl.BlockSpec((tm, d), lambda i: (i, 0)),
        out_shape=jax.ShapeDtypeStruct((t, d), BF16),
        compiler_params=_compiler_params(("parallel",), blocks),
        name="cross_attention",
    )(h, wq, kv, kv)


def _ffn_kernel(h_ref, halo_ref, wg_ref, wv_ref, cw_ref, cb_ref, wd_ref, x_ref, g_ref, xo_ref, ho_ref,
                hext_ref, a0_ref, a1_ref, acc_ref, *, tm, n_tiles, tiles_per_seq):
    i = pl.program_id(0)
    j = pl.program_id(1)
    halo = BF16_ROWS
    acts = (a0_ref, a1_ref)

    def up_gate(slot):
        g = _dot(hext_ref[...], wg_ref[...])
        v = _dot(h_ref[...], wv_ref[...])
        return g, v

    def gate(g, v, slot):
        cw = cw_ref[...]
        conv = cb_ref[...] + cw[FFN_CONV - 1:FFN_CONV, :] * g[halo:, :]
        for kk in range(FFN_CONV - 1):
            off = halo - (FFN_CONV - 1) + kk
            conv = conv + cw[kk:kk + 1, :] * g[off:off + tm, :]
        acts[slot][...] = (jax.nn.silu(conv) * v).astype(BF16)

    def down(slot):
        acc_ref[...] += _dot(acts[slot][...], wd_ref[...])

    @pl.when(j == 0)
    def _():
        keep = (i % tiles_per_seq) != 0
        hext_ref[0:halo, :] = jnp.where(keep, halo_ref[...], jnp.zeros_like(halo_ref))
        hext_ref[halo:, :] = h_ref[...]
        acc_ref[...] = jnp.zeros_like(acc_ref)
        g, v = up_gate(0)
        gate(g, v, 0)

    mid = jnp.logical_and(j > 0, j < n_tiles)

    @pl.when(jnp.logical_and(mid, j % 2 == 1))
    def _():
        g, v = up_gate(1)
        down(0)
        gate(g, v, 1)

    @pl.when(jnp.logical_and(mid, j % 2 == 0))
    def _():
        g, v = up_gate(0)
        down(1)
        gate(g, v, 0)

    @pl.when(j == n_tiles)
    def _():
        down((n_tiles - 1) % 2)
        xn = x_ref[...] + acc_ref[...]
        xo_ref[...] = xn
        ho_ref[...] = _rms(xn, g_ref[...]).astype(ho_ref.dtype)


def ffn(h, w_gate, w_val, conv_w, conv_b, w_down, layer, x, g, seq, h_dtype):
    t, d = h.shape
    n = w_gate.shape[2]
    tn = FFN_TILE_N
    tm = min(512, seq)
    n_tiles = n // tn
    halo = BF16_ROWS
    blocks = 2 * (_nbytes((tm, d), BF16) + 2 * _nbytes((d, tn), BF16) + _nbytes((tn, d), BF16)
                  + 2 * _nbytes((tm, d), F32) + _nbytes((tm, d), h_dtype))
    blocks += _nbytes((tm + halo, d), BF16) + _nbytes((tm, d), F32)
    blocks += 6 * _nbytes((tm, tn), F32) + 2 * _nbytes((tm, d), F32)
    kernel = functools.partial(_ffn_kernel, tm=tm, n_tiles=n_tiles, tiles_per_seq=seq // tm)

    def up_tile(j):
        return jnp.minimum(j, n_tiles - 1)

    def down_tile(j):
        return (j + n_tiles - 1) % n_tiles

    def x_tile(i, j):
        return jnp.where(j >= n_tiles // 2, i, jnp.maximum(i - 1, 0))

    return pl.pallas_call(
        kernel,
        grid=(t // tm, n_tiles + 1),
        in_specs=[pl.BlockSpec((tm, d), lambda i, j: (i, 0)),
                  pl.BlockSpec((halo, d), lambda i, j: (jnp.maximum(i * (tm // halo) - 1, 0), 0)),
                  pl.BlockSpec((None, d, tn), lambda i, j: (layer, 0, up_tile(j))),
                  pl.BlockSpec((None, d, tn), lambda i, j: (layer, 0, up_tile(j))),
                  pl.BlockSpec((None, FFN_CONV, tn), lambda i, j: (layer, 0, up_tile(j))),
                  pl.BlockSpec((None, 1, tn), lambda i, j: (layer, 0, up_tile(j))),
                  pl.BlockSpec((None, tn, d), lambda i, j: (layer, down_tile(j), 0)),
                  pl.BlockSpec((tm, d), lambda i, j: (x_tile(i, j), 0)),
                  pl.BlockSpec((1, d), lambda i, j: (0, 0))],
        out_specs=[pl.BlockSpec((tm, d), lambda i, j: (i, 0)),
                   pl.BlockSpec((tm, d), lambda i, j: (i, 0))],
        out_shape=[jax.ShapeDtypeStruct((t, d), F32), jax.ShapeDtypeStruct((t, d), h_dtype)],
        scratch_shapes=[pltpu.VMEM((tm + halo, d), BF16),
                        pltpu.VMEM((tm, tn), BF16), pltpu.VMEM((tm, tn), BF16),
                        pltpu.VMEM((tm, d), F32)],
        compiler_params=_compiler_params(("parallel", "arbitrary"), blocks),
        name="ffn",
    )(h, h, w_gate, w_val, conv_w, conv_b, w_down, x, g.reshape(1, d))


def _pad_last(a, n):
    return jnp.pad(a, [(0, 0)] * (a.ndim - 1) + [(0, n - a.shape[-1])])


def kernel(x, mem, mem_norm_g, norm_mix_g, w_in, attn_sinks, lru_conv_w, lru_conv_b, lru_wa, lru_ba,
           lru_wx, lru_bx, lru_lambda, w_branch, w_out, norm_xattn_g, xattn_wq, xattn_wkv, xattn_wo,
           norm_ffn_g, ffn_w_up, ffn_conv_w, ffn_conv_b, ffn_w_down, final_norm_g):
    batch, seq, d_model = x.shape
    depth = w_in.shape[0]
    mem_len = mem.shape[1]
    ffn_dim = ffn_w_down.shape[1]
    t = batch * seq
    assert seq % 512 == 0 or seq in (128, 256)
    assert d_model == 2048 and w_in.shape[2] == 13824

    col_ret, col_lru, col_gate = 0, 4 * RET_W, 4 * RET_W + 2 * LRU_WIDTH
    col_q = col_gate + 3 * d_model
    col_k = col_q + SWA_Q_W
    col_v = col_k + SWA_KV_W

    perm_heads = np.asarray(_q_head_order())
    q_cols = (perm_heads[:, None] * SWA_HEAD_DIM + np.arange(SWA_HEAD_DIM)[None, :]).reshape(-1)
    w_in_b = w_in.astype(BF16)
    w_head = jnp.concatenate([w_in[:, :, :SWA_Q_W][:, :, q_cols], w_in[:, :, SWA_Q_W:IN_TILE_N]],
                             axis=2).astype(BF16)
    w_branch_b = w_branch.astype(BF16)
    w_branch_a = w_branch[:, 0][:, q_cols, :].astype(BF16)
    w_out_b = w_out.astype(BF16)
    wq_b = xattn_wq.astype(BF16)
    wkv_b = xattn_wkv.astype(BF16)
    wo_b = xattn_wo.astype(BF16)
    lru_wa_b = lru_wa.astype(BF16)
    lru_wx_b = lru_wx.astype(BF16)
    ffn_pad = -(-ffn_dim // FFN_TILE_N) * FFN_TILE_N
    w_gate = _pad_last(ffn_w_up[:, :, :ffn_dim], ffn_pad).astype(BF16)
    w_val = _pad_last(ffn_w_up[:, :, ffn_dim:], ffn_pad).astype(BF16)
    w_down = jnp.pad(ffn_w_down, ((0, 0), (0, ffn_pad - ffn_dim), (0, 0))).astype(BF16)
    conv_w = _pad_last(ffn_conv_w, ffn_pad)
    conv_b = _pad_last(ffn_conv_b, ffn_pad)[:, None, :]
    cosf, sinf = _rope_tables(seq)

    xf = x.reshape(t, d_model)
    mem_n = rmsnorm(mem.reshape(batch * mem_len, d_model), mem_norm_g)
    h = rmsnorm(xf, norm_mix_g[0])
    for l in range(depth):
        proj = in_proj(h, w_head, w_in_b, l)
        o_a = swa_attention(proj, attn_sinks[l], seq, col_q, col_k, col_v)
        o_b = retention(proj, cosf, sinf, batch, seq, col_ret)
        o_c = rg_lru_block(proj, lru_conv_w[l], lru_conv_b[l], lru_wa_b, lru_ba[l], lru_wx_b,
                           lru_bx[l], lru_lambda[l], l, batch, seq, col_lru)
        merged = branch_merge(o_a, o_b, o_c, w_branch_a, w_branch_b, l, proj, col_gate, d_model)
        xf, h = proj_res_norm(merged, w_out_b, l, xf, norm_xattn_g[l], "mixer_out")

        kv = matmul(mem_n, wkv_b, l, batch * mem_len, 1024, "xattn_kv")
        o_x = cross_attention(h, wq_b, l, kv, seq, mem_len)
        xf, h = proj_res_norm(o_x, wo_b, l, xf, norm_ffn_g[l], "xattn_out")

        last = l == depth - 1
        g_next = final_norm_g if last else norm_mix_g[l + 1]
        xf, h = ffn(h, w_gate, w_val, conv_w, conv_b, w_down, l, xf, g_next, seq,
                    F32 if last else BF16)
    return h.reshape(batch, seq, d_model)
```

```python
import functools

import numpy as np
import jax
import jax.numpy as jnp
from jax import lax
from jax.experimental import pallas as pl
from jax.experimental.pallas import tpu as pltpu

NORM_EPS = 1e-6

LANES = 128
SUBLANES = 8
BF16_ROWS = 16
VMEM_BUDGET_BYTES = 58 * 1024 * 1024

SWA_Q_HEADS = 16
SWA_KV_HEADS = 4
SWA_HEAD_DIM = 64
SWA_BLOCK = 128
SWA_Q_W = SWA_Q_HEADS * SWA_HEAD_DIM
SWA_KV_W = SWA_KV_HEADS * SWA_HEAD_DIM

RET_HEADS = 8
RET_HEAD_DIM = 128
RET_CHUNK = 128
RET_W = RET_HEADS * RET_HEAD_DIM
RET_ROPE_BASE = 10000.0

LRU_WIDTH = 1024
LRU_BLOCKS = 8
LRU_BLOCK_DIM = LRU_WIDTH // LRU_BLOCKS
LRU_CONV = 4
LRU_C = 8.0

XATTN_HEADS = 4

FFN_CONV = 3
FFN_TILE_N = 512
MERGE_CHUNK = 512
IN_TILE_N = SWA_Q_W + 2 * SWA_KV_W

LOG2E = 1.4426950408889634

BF16 = jnp.bfloat16
F32 = jnp.float32


def _compiler_params(semantics, vmem_bytes):
    return pltpu.CompilerParams(
        dimension_semantics=semantics,
        vmem_limit_bytes=int(min(max(vmem_bytes, 16 * 1024 * 1024), VMEM_BUDGET_BYTES)))


def _nbytes(shape, dtype):
    return int(np.prod(shape)) * jnp.dtype(dtype).itemsize


def _dot(a, b):
    return jnp.dot(a, b, preferred_element_type=F32)


def _dot_nt(a, b):
    return lax.dot_general(a, b, (((1,), (1,)), ((), ())), preferred_element_type=F32)


def _rms(x, g):
    ms = jnp.mean(x * x, axis=-1, keepdims=True)
    return x * lax.rsqrt(ms + NORM_EPS) * g


def _rmsnorm_kernel(x_ref, g_ref, o_ref):
    o_ref[...] = _rms(x_ref[...], g_ref[...]).astype(o_ref.dtype)


def rmsnorm(x, g, out_dtype=BF16):
    t, d = x.shape
    tm = min(512, t)
    blocks = 2 * (_nbytes((tm, d), F32) + _nbytes((tm, d), out_dtype)) + 2 * _nbytes((tm, d), F32)
    return pl.pallas_call(
        _rmsnorm_kernel,
        grid=(t // tm,),
        in_specs=[pl.BlockSpec((tm, d), lambda i: (i, 0)),
                  pl.BlockSpec((1, d), lambda i: (0, 0))],
        out_specs=pl.BlockSpec((tm, d), lambda i: (i, 0)),
        out_shape=jax.ShapeDtypeStruct((t, d), out_dtype),
        compiler_params=_compiler_params(("parallel",), blocks),
        name="rmsnorm",
    )(x, g.reshape(1, d))


def _matmul_kernel(a_ref, w_ref, o_ref):
    o_ref[...] = _dot(a_ref[...], w_ref[...]).astype(o_ref.dtype)


def matmul(a, w, layer, tm, tn, name):
    m, k = a.shape
    n = w.shape[2]
    tm = min(tm, m)
    blocks = (2 * (_nbytes((tm, k), BF16) + _nbytes((k, tn), BF16) + _nbytes((tm, tn), BF16))
              + 2 * _nbytes((tm, tn), F32))
    return pl.pallas_call(
        _matmul_kernel,
        grid=(m // tm, n // tn),
        in_specs=[pl.BlockSpec((tm, k), lambda i, j: (i, 0)),
                  pl.BlockSpec((None, k, tn), lambda i, j: (layer, 0, j))],
        out_specs=pl.BlockSpec((tm, tn), lambda i, j: (i, j)),
        out_shape=jax.ShapeDtypeStruct((m, n), BF16),
        compiler_params=_compiler_params(("parallel", "arbitrary"), blocks),
        name=name,
    )(a, w)


def _in_proj_kernel(h_ref, wh_ref, w_ref, o_ref, *, gate_tile0):
    j = pl.program_id(1)

    @pl.when(j == 0)
    def _():
        o_ref[...] = _dot(h_ref[...], wh_ref[...]).astype(o_ref.dtype)

    @pl.when(jnp.logical_and(j > 0, j < gate_tile0))
    def _():
        o_ref[...] = _dot(h_ref[...], w_ref[...]).astype(o_ref.dtype)

    @pl.when(j >= gate_tile0)
    def _():
        o_ref[...] = jax.nn.sigmoid(_dot(h_ref[...], w_ref[...])).astype(o_ref.dtype)


def in_proj(h, w_head, w_in, layer, gate_col0):
    m, k = h.shape
    n = w_in.shape[2]
    tn = IN_TILE_N
    assert w_head.shape[2] == tn
    tm = min(1024, m)
    nt = n // tn
    assert gate_col0 % tn == 0
    blocks = (2 * (_nbytes((tm, k), BF16) + 2 * _nbytes((k, tn), BF16) + _nbytes((tm, tn), BF16))
              + 3 * _nbytes((tm, tn), F32))
    return pl.pallas_call(
        functools.partial(_in_proj_kernel, gate_tile0=gate_col0 // tn),
        grid=(m // tm, nt),
        in_specs=[pl.BlockSpec((tm, k), lambda i, j: (i, 0)),
                  pl.BlockSpec((None, k, tn), lambda i, j: (layer, 0, 0)),
                  pl.BlockSpec((None, k, tn), lambda i, j: (layer, 0, jnp.maximum(j, 1)))],
        out_specs=pl.BlockSpec((tm, tn), lambda i, j: (i, (j + nt - 1) % nt)),
        out_shape=jax.ShapeDtypeStruct((m, n), BF16),
        compiler_params=_compiler_params(("parallel", "arbitrary"), blocks),
        name="in_proj",
    )(h, w_head, w_in)


def _q_head_order():
    order = []
    g = SWA_Q_HEADS // SWA_KV_HEADS
    for kv_pair in range(SWA_KV_HEADS // 2):
        for i in range(g):
            order += [(2 * kv_pair) * g + i, (2 * kv_pair + 1) * g + i]
    return order


def _swa_kernel(sinks_ref, q_ref, kc_ref, vc_ref, kp_ref, vp_ref, o_ref, *, tq, tiles_per_seq):
    i = pl.program_id(0)
    starts_seq = (i % tiles_per_seq) == 0
    blk = SWA_BLOCK
    kwin = jnp.concatenate([kp_ref[...], kc_ref[...]], axis=0)
    vwin = jnp.concatenate([vp_ref[...], vc_ref[...]], axis=0)
    lane = lax.broadcasted_iota(jnp.int32, (blk, LANES), 1)
    lo = lane < SWA_HEAD_DIM
    row = lax.broadcasted_iota(jnp.int32, (2 * blk, 2 * blk), 0)
    col = lax.broadcasted_iota(jnp.int32, (2 * blk, 2 * blk), 1)
    qi = jnp.bitwise_and(row, blk - 1)
    in_window = jnp.logical_and(col > qi, col <= qi + blk)
    in_window_first = jnp.logical_and(in_window, jnp.logical_or(col >= blk, jnp.logical_not(starts_seq)))
    top = lax.broadcasted_iota(jnp.int32, (2 * blk, 1), 0) < blk
    scale = (SWA_HEAD_DIM ** -0.5) * LOG2E
    order = _q_head_order()
    for c in range(tq // blk):
        allowed = in_window_first if c == 0 else in_window
        for p in range(SWA_Q_W // LANES):
            kv_blk = p // (SWA_Q_W // LANES // (SWA_KV_W // LANES))
            kb = kwin[c * blk:(c + 2) * blk, kv_blk * LANES:(kv_blk + 1) * LANES]
            vb = vwin[c * blk:(c + 2) * blk, kv_blk * LANES:(kv_blk + 1) * LANES]
            qb = q_ref[c * blk:(c + 1) * blk, p * LANES:(p + 1) * LANES]
            zero = jnp.zeros_like(qb)
            qs = jnp.concatenate([jnp.where(lo, qb, zero), jnp.where(lo, zero, qb)], axis=0)
            s = _dot_nt(qs, kb) * scale
            s = jnp.where(allowed, s, -jnp.inf)
            sink = jnp.where(top, sinks_ref[order[2 * p]], sinks_ref[order[2 * p + 1]]) * LOG2E
            m = jnp.maximum(jnp.max(s, axis=-1, keepdims=True), sink)
            e = jnp.exp2(s - m)
            denom = jnp.sum(e, axis=-1, keepdims=True) + jnp.exp2(sink - m)
            probs = (e * (1.0 / denom)).astype(BF16)
            o = _dot(probs, vb)
            ob = jnp.where(lo, o[:blk], o[blk:])
            o_ref[c * blk:(c + 1) * blk, p * LANES:(p + 1) * LANES] = ob.astype(o_ref.dtype)


def swa_attention(proj, sinks, seq, col_q, col_k, col_v):
    t = proj.shape[0]
    tq = min(512, seq)
    nsub = tq // SWA_BLOCK
    qb, kb, vb = col_q // SWA_Q_W, col_k // SWA_KV_W, col_v // SWA_KV_W
    blocks = 2 * (2 * _nbytes((tq, SWA_Q_W), BF16) + 2 * _nbytes((tq + SWA_BLOCK, SWA_KV_W), BF16))
    blocks += 16 * _nbytes((2 * SWA_BLOCK, 2 * SWA_BLOCK), F32)
    kernel = functools.partial(_swa_kernel, tq=tq, tiles_per_seq=seq // tq)
    return pl.pallas_call(
        kernel,
        grid=(t // tq,),
        in_specs=[
            pl.BlockSpec(memory_space=pltpu.SMEM),
            pl.BlockSpec((tq, SWA_Q_W), lambda i: (i, qb)),
            pl.BlockSpec((tq, SWA_KV_W), lambda i: (i, kb)),
            pl.BlockSpec((tq, SWA_KV_W), lambda i: (i, vb)),
            pl.BlockSpec((SWA_BLOCK, SWA_KV_W), lambda i: (jnp.maximum(i * nsub - 1, 0), kb)),
            pl.BlockSpec((SWA_BLOCK, SWA_KV_W), lambda i: (jnp.maximum(i * nsub - 1, 0), vb)),
        ],
        out_specs=pl.BlockSpec((tq, SWA_Q_W), lambda i: (i, 0)),
        out_shape=jax.ShapeDtypeStruct((t, SWA_Q_W), BF16),
        compiler_params=_compiler_params(("parallel",), blocks),
        name="swa_attention",
    )(sinks, proj, proj, proj, proj, proj)


_RET_LOG_GAMMA = [float(v) for v in
                  np.log1p(-np.exp2(-5.0 - np.arange(RET_HEADS, dtype=np.float32))).astype(np.float32)]


def _retention_merge_kernel(q_ref, k_ref, v_ref, g_ref, cos_ref, sin_ref, oa_ref, oc_ref, wa_ref,
                            wb_ref, wc_ref, ga_ref, gb_ref, gc_ref, m_ref, ob_scr, state_ref,
                            decay_ref, *, chunks, tiles_per_seq, n_tiles):
    c = RET_CHUNK
    d = RET_HEAD_DIM
    i = pl.program_id(0)
    slot = i % 2
    tile = jnp.minimum(i, n_tiles - 1)

    @pl.when(i == 0)
    def _():
        ob_scr[...] = jnp.zeros_like(ob_scr)
        qpos = lax.broadcasted_iota(jnp.int32, (c, c), 0)
        kpos = lax.broadcasted_iota(jnp.int32, (c, c), 1)
        diff = (qpos - kpos).astype(F32)
        for h in range(RET_HEADS):
            decay_ref[h] = jnp.where(diff >= 0, jnp.exp(jnp.maximum(diff, 0.0) * _RET_LOG_GAMMA[h]), 0.0)

    @pl.when(tile % tiles_per_seq == 0)
    def _():
        state_ref[...] = jnp.zeros_like(state_ref)

    ob_prev = ob_scr.at[1 - slot]
    ob_cur = ob_scr.at[slot]
    m_chunk = MERGE_CHUNK
    heads_per_chunk = RET_HEADS * m_chunk // m_ref.shape[1]
    idx = lax.broadcasted_iota(jnp.int32, (c, 1), 0).astype(F32)
    for h in range(RET_HEADS):
        if h % heads_per_chunk == 0:
            mc = slice(h // heads_per_chunk * m_chunk, (h // heads_per_chunk + 1) * m_chunk)
            merged = ga_ref[:, mc].astype(F32) * _dot(oa_ref[...], wa_ref[:, mc])
            merged = merged + gb_ref[:, mc].astype(F32) * _dot(ob_prev[...], wb_ref[:, mc])
            merged = merged + gc_ref[:, mc].astype(F32) * _dot(oc_ref[...], wc_ref[:, mc])
            m_ref[:, mc] = merged.astype(m_ref.dtype)

        lg = _RET_LOG_GAMMA[h]
        sl = slice(h * d, (h + 1) * d)
        q_decay = jnp.exp((idx + 1.0) * lg)
        k_decay = jnp.exp((c - 1.0 - idx) * lg)
        chunk_decay = float(np.exp(np.float32(c) * np.float32(lg)))
        state = state_ref[h]
        for n in range(chunks):
            rows = slice(n * c, (n + 1) * c)
            cosf = cos_ref[rows, :]
            sinf = sin_ref[rows, :]
            q = q_ref[rows, sl].astype(F32)
            k = k_ref[rows, sl].astype(F32)
            v = v_ref[rows, sl]
            qr = q * cosf + pltpu.roll(q, d // 2, axis=1) * sinf
            kr = (k * cosf + pltpu.roll(k, d // 2, axis=1) * sinf) * (d ** -0.5)
            scores = _dot_nt(qr.astype(BF16), kr.astype(BF16)) * decay_ref[h]
            intra = _dot(scores.astype(BF16), v)
            inter = _dot((qr * q_decay).astype(BF16), state.astype(BF16))
            kd_t = (kr * k_decay).T.astype(BF16)
            state = state * chunk_decay + _dot(kd_t, v)
            y = intra + inter
            mu = jnp.mean(y, axis=-1, keepdims=True)
            yc = y - mu
            yn = yc * lax.rsqrt(jnp.mean(yc * yc, axis=-1, keepdims=True) + NORM_EPS)
            gate = g_ref[rows, sl].astype(F32)
            ob_cur[rows, sl] = (jax.nn.silu(gate) * yn).astype(ob_scr.dtype)
        state_ref[h] = state


def _rope_tables(seq):
    half = RET_HEAD_DIM // 2
    theta = RET_ROPE_BASE ** (-jnp.linspace(0.0, 1.0, half, dtype=F32))
    ang = jnp.arange(seq, dtype=F32)[:, None] * theta[None, :]
    cos = jnp.cos(ang)
    sin = jnp.sin(ang)
    return jnp.concatenate([cos, cos], axis=-1), jnp.concatenate([-sin, sin], axis=-1)


def retention_merge(proj, cosf, sinf, o_a, o_c, w_branch_a, w_branch, layer, seq, col_q, col_gate,
                    d_model):
    t = proj.shape[0]
    c = RET_CHUNK
    tr = min(4 * c, seq)
    nt = seq // tr
    n_tiles = t // tr
    b0 = col_q // RET_W
    g0 = col_gate // d_model
    blocks = 2 * (6 * _nbytes((tr, RET_W), BF16) + 4 * _nbytes((tr, d_model), BF16))
    blocks += 4 * _nbytes((tr, RET_HEAD_DIM), F32) + 3 * _nbytes((RET_W, d_model), BF16)
    blocks += 2 * _nbytes((tr, RET_W), BF16) + 4 * _nbytes((tr, MERGE_CHUNK), F32)
    blocks += 2 * _nbytes((RET_HEADS, RET_HEAD_DIM, RET_HEAD_DIM), F32) + 96 * _nbytes((c, c), F32)
    kernel = functools.partial(_retention_merge_kernel, chunks=tr // c, tiles_per_seq=nt,
                               n_tiles=n_tiles)

    def cur(i):
        return jnp.minimum(i, n_tiles - 1)

    def prev(i):
        return jnp.maximum(i - 1, 0)

    def col(j):
        return pl.BlockSpec((tr, RET_W), lambda i: (cur(i), b0 + j))

    def gate(j):
        return pl.BlockSpec((tr, d_model), lambda i: (prev(i), g0 + j))

    def weight(j):
        return pl.BlockSpec((None, None, RET_W, d_model), lambda i: (layer, j, 0, 0),
                            pipeline_mode=pl.Buffered(1))

    o_prev = pl.BlockSpec((tr, RET_W), lambda i: (prev(i), 0))
    return pl.pallas_call(
        kernel,
        grid=(n_tiles + 1,),
        in_specs=[col(0), col(1), col(2), col(3),
                  pl.BlockSpec((tr, RET_HEAD_DIM), lambda i: (cur(i) % nt, 0)),
                  pl.BlockSpec((tr, RET_HEAD_DIM), lambda i: (cur(i) % nt, 0)),
                  o_prev, o_prev,
                  pl.BlockSpec((None, RET_W, d_model), lambda i: (layer, 0, 0),
                               pipeline_mode=pl.Buffered(1)),
                  weight(1), weight(2), gate(0), gate(1), gate(2)],
        out_specs=pl.BlockSpec((tr, d_model), lambda i: (prev(i), 0)),
        out_shape=jax.ShapeDtypeStruct((t, d_model), BF16),
        scratch_shapes=[pltpu.VMEM((2, tr, RET_W), BF16),
                        pltpu.VMEM((RET_HEADS, RET_HEAD_DIM, RET_HEAD_DIM), F32),
                        pltpu.VMEM((RET_HEADS, c, c), F32)],
        compiler_params=_compiler_params(("arbitrary",), blocks),
        name="retention_merge",
    )(proj, proj, proj, proj, cosf, sinf, o_a, o_c, w_branch_a, w_branch, w_branch, proj, proj, proj)


def _lru_kernel(x_ref, y_ref, cw_ref, cb_ref, wa_ref, ba_ref, wx_ref, bx_ref, lam_ref, o_ref,
                xprev_ref, carry_ref, *, tt):
    w = LRU_WIDTH

    @pl.when(pl.program_id(1) == 0)
    def _():
        xprev_ref[...] = jnp.zeros_like(xprev_ref)
        carry_ref[...] = jnp.zeros_like(carry_ref)

    x = x_ref[...].astype(F32)
    xe = jnp.concatenate([xprev_ref[...], x], axis=0)
    xprev_ref[...] = x[tt - SUBLANES:, :]
    cw = cw_ref[...]
    xc = cb_ref[...] + cw[3:4, :] * x
    for kk in range(LRU_CONV - 1):
        off = SUBLANES - (LRU_CONV - 1) + kk
        xc = xc + cw[kk:kk + 1, :] * xe[off:off + tt, :]

    xcb = xc.astype(BF16)
    gr = []
    gi = []
    for blk in range(LRU_BLOCKS):
        sl = slice(blk * LRU_BLOCK_DIM, (blk + 1) * LRU_BLOCK_DIM)
        gr.append(_dot(xcb[:, sl], wa_ref[blk]))
        gi.append(_dot(xcb[:, sl], wx_ref[blk]))
    gate_r = jax.nn.sigmoid(jnp.concatenate(gr, axis=1) + ba_ref[...])
    gate_i = jax.nn.sigmoid(jnp.concatenate(gi, axis=1) + bx_ref[...])
    z = -lam_ref[...]
    softplus = jnp.maximum(z, 0.0) + jnp.log1p(jnp.exp(-jnp.abs(z)))
    log_a = (-LRU_C) * gate_r * softplus
    a = jnp.exp(log_a)
    u = jnp.sqrt(1.0 - a * a) * (gate_i * xc)

    nv = tt // SUBLANES
    a3 = a.reshape(nv, SUBLANES, w)
    b3 = u.reshape(nv, SUBLANES, w)
    rix = lax.broadcasted_iota(jnp.int32, (nv, SUBLANES, w), 1)
    s = 1
    while s < SUBLANES:
        keep = rix >= s
        a_sh = jnp.where(keep, pltpu.roll(a3, s, axis=1), 1.0)
        b_sh = jnp.where(keep, pltpu.roll(b3, s, axis=1), 0.0)
        b3 = a3 * b_sh + b3
        a3 = a3 * a_sh
        s *= 2
    carry = carry_ref[...]
    hs = []
    for i in range(nv):
        h_i = a3[i] * carry + b3[i]
        carry = h_i[SUBLANES - 1:SUBLANES, :]
        hs.append(h_i)
    carry_ref[...] = carry
    h = jnp.concatenate(hs, axis=0)
    o_ref[...] = (h * jax.nn.gelu(y_ref[...].astype(F32))).astype(o_ref.dtype)


def rg_lru_block(proj, conv_w, conv_b, wa, ba, wx, bx, lam, layer, batch, seq, col_x):
    t = proj.shape[0]
    w = LRU_WIDTH
    tt = min(256, seq)
    nt = seq // tt
    b0 = col_x // w
    blocks = 2 * 3 * _nbytes((tt, w), BF16) + 24 * _nbytes((tt, w), F32)
    blocks += 4 * _nbytes((LRU_BLOCKS, LRU_BLOCK_DIM, LRU_BLOCK_DIM), BF16)
    kernel = functools.partial(_lru_kernel, tt=tt)
    vec = pl.BlockSpec((1, w), lambda b, n: (0, 0))
    blockdiag = pl.BlockSpec((None, LRU_BLOCKS, LRU_BLOCK_DIM, LRU_BLOCK_DIM),
                             lambda b, n: (layer, 0, 0, 0))
    return pl.pallas_call(
        kernel,
        grid=(batch, nt),
        in_specs=[pl.BlockSpec((tt, w), lambda b, n: (b * nt + n, b0)),
                  pl.BlockSpec((tt, w), lambda b, n: (b * nt + n, b0 + 1)),
                  pl.BlockSpec((LRU_CONV, w), lambda b, n: (0, 0)),
                  vec, blockdiag, vec, blockdiag, vec, vec],
        out_specs=pl.BlockSpec((tt, w), lambda b, n: (b * nt + n, 0)),
        out_shape=jax.ShapeDtypeStruct((t, w), BF16),
        scratch_shapes=[pltpu.VMEM((SUBLANES, w), F32), pltpu.VMEM((1, w), F32)],
        compiler_params=_compiler_params(("arbitrary", "arbitrary"), blocks),
        name="rg_lru",
    )(proj, proj, conv_w, conv_b.reshape(1, w), wa, ba.reshape(1, w), wx, bx.reshape(1, w),
      lam.reshape(1, w))


def _proj_res_norm_kernel(a_ref, w_ref, x_ref, g_ref, xo_ref, ho_ref):
    xn = x_ref[...] + _dot(a_ref[...], w_ref[...])
    xo_ref[...] = xn
    ho_ref[...] = _rms(xn, g_ref[...]).astype(ho_ref.dtype)


def proj_res_norm(a, w, layer, x, g, name):
    t, kdim = a.shape
    d = w.shape[2]
    tm = min(512, t)
    blocks = 2 * (_nbytes((tm, kdim), BF16) + _nbytes((kdim, d), BF16) + 2 * _nbytes((tm, d), F32)
                  + _nbytes((tm, d), BF16)) + 3 * _nbytes((tm, d), F32)
    return pl.pallas_call(
        _proj_res_norm_kernel,
        grid=(t // tm,),
        in_specs=[pl.BlockSpec((tm, kdim), lambda i: (i, 0)),
                  pl.BlockSpec((None, kdim, d), lambda i: (layer, 0, 0)),
                  pl.BlockSpec((tm, d), lambda i: (i, 0)),
                  pl.BlockSpec((1, d), lambda i: (0, 0))],
        out_specs=[pl.BlockSpec((tm, d), lambda i: (i, 0)),
                   pl.BlockSpec((tm, d), lambda i: (i, 0))],
        out_shape=[jax.ShapeDtypeStruct((t, d), F32), jax.ShapeDtypeStruct((t, d), BF16)],
        compiler_params=_compiler_params(("parallel",), blocks),
        name=name,
    )(a, w, x, g.reshape(1, d))


def _xattn_kernel(h_ref, wq_ref, k_ref, v_ref, o_ref, *, head_dim):
    q = _dot(h_ref[...], wq_ref[...]).astype(BF16)
    scale = (head_dim ** -0.5) * LOG2E
    for hd in range(XATTN_HEADS):
        sl = slice(hd * head_dim, (hd + 1) * head_dim)
        s = _dot_nt(q[:, sl], k_ref[:, sl]) * scale
        m = jnp.max(s, axis=-1, keepdims=True)
        e = jnp.exp2(s - m)
        p = (e * (1.0 / jnp.sum(e, axis=-1, keepdims=True))).astype(BF16)
        o_ref[:, sl] = _dot(p, v_ref[:, sl]).astype(o_ref.dtype)


def cross_attention(h, wq, layer, kv, seq, mem_len):
    t, d = h.shape
    tm = min(512, seq)
    tiles_per_seq = seq // tm
    blocks = 2 * (2 * _nbytes((tm, d), BF16) + _nbytes((d, d), BF16) + 2 * _nbytes((mem_len, d), BF16))
    blocks += 3 * _nbytes((tm, d), F32)
    kernel = functools.partial(_xattn_kernel, head_dim=d // XATTN_HEADS)
    return pl.pallas_call(
        kernel,
        grid=(t // tm,),
        in_specs=[pl.BlockSpec((tm, d), lambda i: (i, 0)),
                  pl.BlockSpec((None, d, d), lambda i: (layer, 0, 0)),
                  pl.BlockSpec((mem_len, d), lambda i: (i // tiles_per_seq, 0)),
                  pl.BlockSpec((mem_len, d), lambda i: (i // tiles_per_seq, 1))],
        out_specs=pl.BlockSpec((tm, d), lambda i: (i, 0)),
        out_shape=jax.ShapeDtypeStruct((t, d), BF16),
        compiler_params=_compiler_params(("parallel",), blocks),
        name="cross_attention",
    )(h, wq, kv, kv)


def _ffn_kernel(h_ref, halo_ref, wg_ref, wv_ref, cw_ref, cb_ref, wd_ref, x_ref, g_ref, xo_ref, ho_ref,
                hext_ref, acc_ref, *, tm, n_tiles, tiles_per_seq):
    i = pl.program_id(0)
    j = pl.program_id(1)
    halo = BF16_ROWS

    @pl.when(j == 0)
    def _():
        keep = (i % tiles_per_seq) != 0
        hext_ref[0:halo, :] = jnp.where(keep, halo_ref[...], jnp.zeros_like(halo_ref))
        hext_ref[halo:, :] = h_ref[...]
        acc_ref[...] = jnp.zeros_like(acc_ref)

    g = _dot(hext_ref[...], wg_ref[...])
    v = _dot(h_ref[...], wv_ref[...])
    cw = cw_ref[...]
    conv = cb_ref[...] + cw[FFN_CONV - 1:FFN_CONV, :] * g[halo:, :]
    for kk in range(FFN_CONV - 1):
        off = halo - (FFN_CONV - 1) + kk
        conv = conv + cw[kk:kk + 1, :] * g[off:off + tm, :]
    act = (jax.nn.silu(conv) * v).astype(BF16)
    acc_ref[...] += _dot(act, wd_ref[...])

    @pl.when(j == n_tiles - 1)
    def _():
        xn = x_ref[...] + acc_ref[...]
        xo_ref[...] = xn
        ho_ref[...] = _rms(xn, g_ref[...]).astype(ho_ref.dtype)


def ffn(h, w_gate, w_val, conv_w, conv_b, w_down, layer, x, g, seq, h_dtype):
    t, d = h.shape
    n = w_gate.shape[2]
    tn = FFN_TILE_N
    tm = min(512, seq)
    n_tiles = n // tn
    halo = BF16_ROWS
    blocks = 2 * (_nbytes((tm, d), BF16) + 2 * _nbytes((d, tn), BF16) + _nbytes((tn, d), BF16)
                  + 2 * _nbytes((tm, d), F32) + _nbytes((tm, d), h_dtype))
    blocks += _nbytes((tm + halo, d), BF16) + _nbytes((tm, d), F32)
    blocks += 5 * _nbytes((tm, tn), F32) + 2 * _nbytes((tm, d), F32)
    kernel = functools.partial(_ffn_kernel, tm=tm, n_tiles=n_tiles, tiles_per_seq=seq // tm)
    return pl.pallas_call(
        kernel,
        grid=(t // tm, n_tiles),
        in_specs=[pl.BlockSpec((tm, d), lambda i, j: (i, 0)),
                  pl.BlockSpec((halo, d), lambda i, j: (jnp.maximum(i * (tm // halo) - 1, 0), 0)),
                  pl.BlockSpec((None, d, tn), lambda i, j: (layer, 0, j)),
                  pl.BlockSpec((None, d, tn), lambda i, j: (layer, 0, j)),
                  pl.BlockSpec((None, FFN_CONV, tn), lambda i, j: (layer, 0, j)),
                  pl.BlockSpec((None, 1, tn), lambda i, j: (layer, 0, j)),
                  pl.BlockSpec((None, tn, d), lambda i, j: (layer, j, 0)),
                  pl.BlockSpec((tm, d), lambda i, j: (i, 0)),
                  pl.BlockSpec((1, d), lambda i, j: (0, 0))],
        out_specs=[pl.BlockSpec((tm, d), lambda i, j: (i, 0)),
                   pl.BlockSpec((tm, d), lambda i, j: (i, 0))],
        out_shape=[jax.ShapeDtypeStruct((t, d), F32), jax.ShapeDtypeStruct((t, d), h_dtype)],
        scratch_shapes=[pltpu.VMEM((tm + halo, d), BF16), pltpu.VMEM((tm, d), F32)],
        compiler_params=_compiler_params(("parallel", "arbitrary"), blocks),
        name="ffn",
    )(h, h, w_gate, w_val, conv_w, conv_b, w_down, x, g.reshape(1, d))


def _pad_last(a, n):
    return jnp.pad(a, [(0, 0)] * (a.ndim - 1) + [(0, n - a.shape[-1])])


def kernel(x, mem, mem_norm_g, norm_mix_g, w_in, attn_sinks, lru_conv_w, lru_conv_b, lru_wa, lru_ba,
           lru_wx, lru_bx, lru_lambda, w_branch, w_out, norm_xattn_g, xattn_wq, xattn_wkv, xattn_wo,
           norm_ffn_g, ffn_w_up, ffn_conv_w, ffn_conv_b, ffn_w_down, final_norm_g):
    batch, seq, d_model = x.shape
    depth = w_in.shape[0]
    mem_len = mem.shape[1]
    ffn_dim = ffn_w_down.shape[1]
    t = batch * seq
    assert seq % 512 == 0 or seq in (128, 256)
    assert d_model == 2048 and w_in.shape[2] == 13824

    col_ret, col_lru, col_gate = 0, 4 * RET_W, 4 * RET_W + 2 * LRU_WIDTH
    col_q = col_gate + 3 * d_model
    col_k = col_q + SWA_Q_W
    col_v = col_k + SWA_KV_W

    perm_heads = np.asarray(_q_head_order())
    q_cols = (perm_heads[:, None] * SWA_HEAD_DIM + np.arange(SWA_HEAD_DIM)[None, :]).reshape(-1)
    w_in_b = w_in.astype(BF16)
    w_head = jnp.concatenate([w_in[:, :, :SWA_Q_W][:, :, q_cols], w_in[:, :, SWA_Q_W:IN_TILE_N]],
                             axis=2).astype(BF16)
    w_branch_b = w_branch.astype(BF16)
    w_branch_a = w_branch[:, 0][:, q_cols, :].astype(BF16)
    w_out_b = w_out.astype(BF16)
    wq_b = xattn_wq.astype(BF16)
    wkv_b = xattn_wkv.astype(BF16)
    wo_b = xattn_wo.astype(BF16)
    lru_wa_b = lru_wa.astype(BF16)
    lru_wx_b = lru_wx.astype(BF16)
    ffn_pad = -(-ffn_dim // FFN_TILE_N) * FFN_TILE_N
    w_gate = _pad_last(ffn_w_up[:, :, :ffn_dim], ffn_pad).astype(BF16)
    w_val = _pad_last(ffn_w_up[:, :, ffn_dim:], ffn_pad).astype(BF16)
    w_down = jnp.pad(ffn_w_down, ((0, 0), (0, ffn_pad - ffn_dim), (0, 0))).astype(BF16)
    conv_w = _pad_last(ffn_conv_w, ffn_pad)
    conv_b = _pad_last(ffn_conv_b, ffn_pad)[:, None, :]
    cosf, sinf = _rope_tables(seq)

    xf = x.reshape(t, d_model)
    mem_n = rmsnorm(mem.reshape(batch * mem_len, d_model), mem_norm_g)
    h = rmsnorm(xf, norm_mix_g[0])
    for l in range(depth):
        proj = in_proj(h, w_head, w_in_b, l, w_in.shape[2] - 3 * d_model)
        o_a = swa_attention(proj, attn_sinks[l], seq, col_q, col_k, col_v)
        o_c = rg_lru_block(proj, lru_conv_w[l], lru_conv_b[l], lru_wa_b, lru_ba[l], lru_wx_b,
                           lru_bx[l], lru_lambda[l], l, batch, seq, col_lru)
        merged = retention_merge(proj, cosf, sinf, o_a, o_c, w_branch_a, w_branch_b, l, seq, col_ret,
                                 col_gate, d_model)
        xf, h = proj_res_norm(merged, w_out_b, l, xf, norm_xattn_g[l], "mixer_out")

        kv = matmul(mem_n, wkv_b, l, batch * mem_len, 1024, "xattn_kv")
        o_x = cross_attention(h, wq_b, l, kv, seq, mem_len)
        xf, h = proj_res_norm(o_x, wo_b, l, xf, norm_ffn_g[l], "xattn_out")

        last = l == depth - 1
        g_next = final_norm_g if last else norm_mix_g[l + 1]
        xf, h = ffn(h, w_gate, w_val, conv_w, conv_b, w_down, l, xf, g_next, seq,
                    F32 if last else BF16)
    return h.reshape(batch, seq, d_model)
```

```python
import functools

import numpy as np
import jax
import jax.numpy as jnp
from jax import lax
from jax.experimental import pallas as pl
from jax.experimental.pallas import tpu as pltpu

NORM_EPS = 1e-6

LANES = 128
SUBLANES = 8
BF16_ROWS = 16
VMEM_BUDGET_BYTES = 58 * 1024 * 1024

SWA_Q_HEADS = 16
SWA_KV_HEADS = 4
SWA_HEAD_DIM = 64
SWA_BLOCK = 128
SWA_Q_W = SWA_Q_HEADS * SWA_HEAD_DIM
SWA_KV_W = SWA_KV_HEADS * SWA_HEAD_DIM

RET_HEADS = 8
RET_HEAD_DIM = 128
RET_CHUNK = 128
RET_W = RET_HEADS * RET_HEAD_DIM
RET_ROPE_BASE = 10000.0

LRU_WIDTH = 1024
LRU_BLOCKS = 8
LRU_BLOCK_DIM = LRU_WIDTH // LRU_BLOCKS
LRU_CONV = 4
LRU_C = 8.0

XATTN_HEADS = 4

FFN_CONV = 3
FFN_TILE_N = 512
MERGE_CHUNK = 1024
IN_TILE_N = SWA_Q_W + 2 * SWA_KV_W

LOG2E = 1.4426950408889634

BF16 = jnp.bfloat16
F32 = jnp.float32


def _compiler_params(semantics, vmem_bytes):
    return pltpu.CompilerParams(
        dimension_semantics=semantics,
        vmem_limit_bytes=int(min(max(vmem_bytes, 16 * 1024 * 1024), VMEM_BUDGET_BYTES)))


def _nbytes(shape, dtype):
    return int(np.prod(shape)) * jnp.dtype(dtype).itemsize


def _dot(a, b):
    return jnp.dot(a, b, preferred_element_type=F32)


def _dot_nt(a, b):
    return lax.dot_general(a, b, (((1,), (1,)), ((), ())), preferred_element_type=F32)


def _rms(x, g):
    ms = jnp.mean(x * x, axis=-1, keepdims=True)
    return x * lax.rsqrt(ms + NORM_EPS) * g


def _rmsnorm_kernel(x_ref, g_ref, o_ref):
    o_ref[...] = _rms(x_ref[...], g_ref[...]).astype(o_ref.dtype)


def rmsnorm(x, g, out_dtype=BF16):
    t, d = x.shape
    tm = min(512, t)
    blocks = 2 * (_nbytes((tm, d), F32) + _nbytes((tm, d), out_dtype)) + 2 * _nbytes((tm, d), F32)
    return pl.pallas_call(
        _rmsnorm_kernel,
        grid=(t // tm,),
        in_specs=[pl.BlockSpec((tm, d), lambda i: (i, 0)),
                  pl.BlockSpec((1, d), lambda i: (0, 0))],
        out_specs=pl.BlockSpec((tm, d), lambda i: (i, 0)),
        out_shape=jax.ShapeDtypeStruct((t, d), out_dtype),
        compiler_params=_compiler_params(("parallel",), blocks),
        name="rmsnorm",
    )(x, g.reshape(1, d))


def _matmul_kernel(a_ref, w_ref, o_ref):
    o_ref[...] = _dot(a_ref[...], w_ref[...]).astype(o_ref.dtype)


def matmul(a, w, layer, tm, tn, name):
    m, k = a.shape
    n = w.shape[2]
    tm = min(tm, m)
    blocks = (2 * (_nbytes((tm, k), BF16) + _nbytes((k, tn), BF16) + _nbytes((tm, tn), BF16))
              + 2 * _nbytes((tm, tn), F32))
    return pl.pallas_call(
        _matmul_kernel,
        grid=(m // tm, n // tn),
        in_specs=[pl.BlockSpec((tm, k), lambda i, j: (i, 0)),
                  pl.BlockSpec((None, k, tn), lambda i, j: (layer, 0, j))],
        out_specs=pl.BlockSpec((tm, tn), lambda i, j: (i, j)),
        out_shape=jax.ShapeDtypeStruct((m, n), BF16),
        compiler_params=_compiler_params(("parallel", "arbitrary"), blocks),
        name=name,
    )(a, w)


def _in_proj_kernel(h_ref, wh_ref, w_ref, o_ref):
    j = pl.program_id(1)

    @pl.when(j == 0)
    def _():
        o_ref[...] = _dot(h_ref[...], wh_ref[...]).astype(o_ref.dtype)

    @pl.when(j > 0)
    def _():
        o_ref[...] = _dot(h_ref[...], w_ref[...]).astype(o_ref.dtype)


def in_proj(h, w_head, w_in, layer):
    m, k = h.shape
    n = w_in.shape[2]
    tn = IN_TILE_N
    assert w_head.shape[2] == tn
    tm = min(1024, m)
    nt = n // tn
    blocks = (2 * (_nbytes((tm, k), BF16) + 2 * _nbytes((k, tn), BF16) + _nbytes((tm, tn), BF16))
              + 2 * _nbytes((tm, tn), F32))
    return pl.pallas_call(
        _in_proj_kernel,
        grid=(m // tm, nt),
        in_specs=[pl.BlockSpec((tm, k), lambda i, j: (i, 0)),
                  pl.BlockSpec((None, k, tn), lambda i, j: (layer, 0, 0)),
                  pl.BlockSpec((None, k, tn), lambda i, j: (layer, 0, jnp.maximum(j, 1)))],
        out_specs=pl.BlockSpec((tm, tn), lambda i, j: (i, (j + nt - 1) % nt)),
        out_shape=jax.ShapeDtypeStruct((m, n), BF16),
        compiler_params=_compiler_params(("parallel", "arbitrary"), blocks),
        name="in_proj",
    )(h, w_head, w_in)


def _q_head_order():
    order = []
    g = SWA_Q_HEADS // SWA_KV_HEADS
    for kv_pair in range(SWA_KV_HEADS // 2):
        for i in range(g):
            order += [(2 * kv_pair) * g + i, (2 * kv_pair + 1) * g + i]
    return order


def _swa_kernel(sinks_ref, q_ref, kc_ref, vc_ref, kp_ref, vp_ref, o_ref, *, tq, tiles_per_seq):
    i = pl.program_id(0)
    starts_seq = (i % tiles_per_seq) == 0
    blk = SWA_BLOCK
    kwin = jnp.concatenate([kp_ref[...], kc_ref[...]], axis=0)
    vwin = jnp.concatenate([vp_ref[...], vc_ref[...]], axis=0)
    lane = lax.broadcasted_iota(jnp.int32, (blk, LANES), 1)
    lo = lane < SWA_HEAD_DIM
    row = lax.broadcasted_iota(jnp.int32, (2 * blk, 2 * blk), 0)
    col = lax.broadcasted_iota(jnp.int32, (2 * blk, 2 * blk), 1)
    qi = jnp.bitwise_and(row, blk - 1)
    in_window = jnp.logical_and(col > qi, col <= qi + blk)
    in_window_first = jnp.logical_and(in_window, jnp.logical_or(col >= blk, jnp.logical_not(starts_seq)))
    top = lax.broadcasted_iota(jnp.int32, (2 * blk, 1), 0) < blk
    scale = (SWA_HEAD_DIM ** -0.5) * LOG2E
    order = _q_head_order()
    for c in range(tq // blk):
        allowed = in_window_first if c == 0 else in_window
        for p in range(SWA_Q_W // LANES):
            kv_blk = p // (SWA_Q_W // LANES // (SWA_KV_W // LANES))
            kb = kwin[c * blk:(c + 2) * blk, kv_blk * LANES:(kv_blk + 1) * LANES]
            vb = vwin[c * blk:(c + 2) * blk, kv_blk * LANES:(kv_blk + 1) * LANES]
            qb = q_ref[c * blk:(c + 1) * blk, p * LANES:(p + 1) * LANES]
            zero = jnp.zeros_like(qb)
            qs = jnp.concatenate([jnp.where(lo, qb, zero), jnp.where(lo, zero, qb)], axis=0)
            s = _dot_nt(qs, kb) * scale
            s = jnp.where(allowed, s, -jnp.inf)
            sink = jnp.where(top, sinks_ref[order[2 * p]], sinks_ref[order[2 * p + 1]]) * LOG2E
            m = jnp.maximum(jnp.max(s, axis=-1, keepdims=True), sink)
            e = jnp.exp2(s - m)
            denom = jnp.sum(e, axis=-1, keepdims=True) + jnp.exp2(sink - m)
            probs = (e * (1.0 / denom)).astype(BF16)
            o = _dot(probs, vb)
            ob = jnp.where(lo, o[:blk], o[blk:])
            o_ref[c * blk:(c + 1) * blk, p * LANES:(p + 1) * LANES] = ob.astype(o_ref.dtype)


def swa_attention(proj, sinks, seq, col_q, col_k, col_v):
    t = proj.shape[0]
    tq = min(512, seq)
    nsub = tq // SWA_BLOCK
    qb, kb, vb = col_q // SWA_Q_W, col_k // SWA_KV_W, col_v // SWA_KV_W
    blocks = 2 * (2 * _nbytes((tq, SWA_Q_W), BF16) + 2 * _nbytes((tq + SWA_BLOCK, SWA_KV_W), BF16))
    blocks += 16 * _nbytes((2 * SWA_BLOCK, 2 * SWA_BLOCK), F32)
    kernel = functools.partial(_swa_kernel, tq=tq, tiles_per_seq=seq // tq)
    return pl.pallas_call(
        kernel,
        grid=(t // tq,),
        in_specs=[
            pl.BlockSpec(memory_space=pltpu.SMEM),
            pl.BlockSpec((tq, SWA_Q_W), lambda i: (i, qb)),
            pl.BlockSpec((tq, SWA_KV_W), lambda i: (i, kb)),
            pl.BlockSpec((tq, SWA_KV_W), lambda i: (i, vb)),
            pl.BlockSpec((SWA_BLOCK, SWA_KV_W), lambda i: (jnp.maximum(i * nsub - 1, 0), kb)),
            pl.BlockSpec((SWA_BLOCK, SWA_KV_W), lambda i: (jnp.maximum(i * nsub - 1, 0), vb)),
        ],
        out_specs=pl.BlockSpec((tq, SWA_Q_W), lambda i: (i, 0)),
        out_shape=jax.ShapeDtypeStruct((t, SWA_Q_W), BF16),
        compiler_params=_compiler_params(("parallel",), blocks),
        name="swa_attention",
    )(sinks, proj, proj, proj, proj, proj)


_RET_LOG_GAMMA = [float(v) for v in
                  np.log1p(-np.exp2(-5.0 - np.arange(RET_HEADS, dtype=np.float32))).astype(np.float32)]


def _retention_merge_kernel(q_ref, k_ref, v_ref, g_ref, cos_ref, sin_ref, oa_ref, oc_ref, wa_ref,
                            wb_ref, wc_ref, ga_ref, gb_ref, gc_ref, m_ref, ob_scr, state_ref,
                            decay_ref, *, chunks, tiles_per_seq, n_tiles):
    c = RET_CHUNK
    d = RET_HEAD_DIM
    i = pl.program_id(0)
    slot = i % 2
    tile = jnp.minimum(i, n_tiles - 1)

    @pl.when(i == 0)
    def _():
        ob_scr[...] = jnp.zeros_like(ob_scr)
        qpos = lax.broadcasted_iota(jnp.int32, (c, c), 0)
        kpos = lax.broadcasted_iota(jnp.int32, (c, c), 1)
        diff = (qpos - kpos).astype(F32)
        for h in range(RET_HEADS):
            decay_ref[h] = jnp.where(diff >= 0, jnp.exp(jnp.maximum(diff, 0.0) * _RET_LOG_GAMMA[h]), 0.0)

    @pl.when(tile % tiles_per_seq == 0)
    def _():
        state_ref[...] = jnp.zeros_like(state_ref)

    ob_prev = ob_scr.at[1 - slot]
    ob_cur = ob_scr.at[slot]
    m_chunk = MERGE_CHUNK
    heads_per_chunk = RET_HEADS * m_chunk // m_ref.shape[1]
    idx = lax.broadcasted_iota(jnp.int32, (c, 1), 0).astype(F32)
    for h in range(RET_HEADS):
        if h % heads_per_chunk == 0:
            mc = slice(h // heads_per_chunk * m_chunk, (h // heads_per_chunk + 1) * m_chunk)
            def gated(g_ref, o, w_ref):
                return jax.nn.sigmoid(g_ref[:, mc].astype(F32)) * _dot(o, w_ref[:, mc])

            merged = gated(ga_ref, oa_ref[...], wa_ref)
            merged = merged + gated(gb_ref, ob_prev[...], wb_ref)
            merged = merged + gated(gc_ref, oc_ref[...], wc_ref)
            m_ref[:, mc] = merged.astype(m_ref.dtype)

        lg = _RET_LOG_GAMMA[h]
        sl = slice(h * d, (h + 1) * d)
        q_decay = jnp.exp((idx + 1.0) * lg)
        k_decay = jnp.exp((c - 1.0 - idx) * lg)
        chunk_decay = float(np.exp(np.float32(c) * np.float32(lg)))
        state = state_ref[h]
        for n in range(chunks):
            rows = slice(n * c, (n + 1) * c)
            cosf = cos_ref[rows, :]
            sinf = sin_ref[rows, :]
            q = q_ref[rows, sl].astype(F32)
            k = k_ref[rows, sl].astype(F32)
            v = v_ref[rows, sl]
            qr = q * cosf + pltpu.roll(q, d // 2, axis=1) * sinf
            kr = (k * cosf + pltpu.roll(k, d // 2, axis=1) * sinf) * (d ** -0.5)
            scores = _dot_nt(qr.astype(BF16), kr.astype(BF16)) * decay_ref[h]
            intra = _dot(scores.astype(BF16), v)
            inter = _dot((qr * q_decay).astype(BF16), state.astype(BF16))
            kd_t = (kr * k_decay).T.astype(BF16)
            state = state * chunk_decay + _dot(kd_t, v)
            y = intra + inter
            mu = jnp.mean(y, axis=-1, keepdims=True)
            yc = y - mu
            yn = yc * lax.rsqrt(jnp.mean(yc * yc, axis=-1, keepdims=True) + NORM_EPS)
            gate = g_ref[rows, sl].astype(F32)
            ob_cur[rows, sl] = (jax.nn.silu(gate) * yn).astype(ob_scr.dtype)
        state_ref[h] = state


def _rope_tables(seq):
    half = RET_HEAD_DIM // 2
    theta = RET_ROPE_BASE ** (-jnp.linspace(0.0, 1.0, half, dtype=F32))
    ang = jnp.arange(seq, dtype=F32)[:, None] * theta[None, :]
    cos = jnp.cos(ang)
    sin = jnp.sin(ang)
    return jnp.concatenate([cos, cos], axis=-1), jnp.concatenate([-sin, sin], axis=-1)


def retention_merge(proj, cosf, sinf, o_a, o_c, w_branch_a, w_branch, layer, seq, col_q, col_gate,
                    d_model):
    t = proj.shape[0]
    c = RET_CHUNK
    tr = min(4 * c, seq)
    nt = seq // tr
    n_tiles = t // tr
    b0 = col_q // RET_W
    g0 = col_gate // d_model
    blocks = 2 * (6 * _nbytes((tr, RET_W), BF16) + 4 * _nbytes((tr, d_model), BF16))
    blocks += 4 * _nbytes((tr, RET_HEAD_DIM), F32) + 3 * _nbytes((RET_W, d_model), BF16)
    blocks += 2 * _nbytes((tr, RET_W), BF16) + 4 * _nbytes((tr, MERGE_CHUNK), F32)
    blocks += 2 * _nbytes((RET_HEADS, RET_HEAD_DIM, RET_HEAD_DIM), F32) + 96 * _nbytes((c, c), F32)
    kernel = functools.partial(_retention_merge_kernel, chunks=tr // c, tiles_per_seq=nt,
                               n_tiles=n_tiles)

    def cur(i):
        return jnp.minimum(i, n_tiles - 1)

    def prev(i):
        return jnp.maximum(i - 1, 0)

    def col(j):
        return pl.BlockSpec((tr, RET_W), lambda i: (cur(i), b0 + j))

    def gate(j):
        return pl.BlockSpec((tr, d_model), lambda i: (prev(i), g0 + j))

    def weight(j):
        return pl.BlockSpec((None, None, RET_W, d_model), lambda i: (layer, j, 0, 0),
                            pipeline_mode=pl.Buffered(1))

    o_prev = pl.BlockSpec((tr, RET_W), lambda i: (prev(i), 0))
    return pl.pallas_call(
        kernel,
        grid=(n_tiles + 1,),
        in_specs=[col(0), col(1), col(2), col(3),
                  pl.BlockSpec((tr, RET_HEAD_DIM), lambda i: (cur(i) % nt, 0)),
                  pl.BlockSpec((tr, RET_HEAD_DIM), lambda i: (cur(i) % nt, 0)),
                  o_prev, o_prev,
                  pl.BlockSpec((None, RET_W, d_model), lambda i: (layer, 0, 0),
                               pipeline_mode=pl.Buffered(1)),
                  weight(1), weight(2), gate(0), gate(1), gate(2)],
        out_specs=pl.BlockSpec((tr, d_model), lambda i: (prev(i), 0)),
        out_shape=jax.ShapeDtypeStruct((t, d_model), BF16),
        scratch_shapes=[pltpu.VMEM((2, tr, RET_W), BF16),
                        pltpu.VMEM((RET_HEADS, RET_HEAD_DIM, RET_HEAD_DIM), F32),
                        pltpu.VMEM((RET_HEADS, c, c), F32)],
        compiler_params=_compiler_params(("arbitrary",), blocks),
        name="retention_merge",
    )(proj, proj, proj, proj, cosf, sinf, o_a, o_c, w_branch_a, w_branch, w_branch, proj, proj, proj)


def _lru_kernel(x_ref, y_ref, cw_ref, cb_ref, wa_ref, ba_ref, wx_ref, bx_ref, lam_ref, o_ref,
                xprev_ref, carry_ref, *, tt):
    w = LRU_WIDTH

    @pl.when(pl.program_id(1) == 0)
    def _():
        xprev_ref[...] = jnp.zeros_like(xprev_ref)
        carry_ref[...] = jnp.zeros_like(carry_ref)

    x = x_ref[...].astype(F32)
    xe = jnp.concatenate([xprev_ref[...], x], axis=0)
    xprev_ref[...] = x[tt - SUBLANES:, :]
    cw = cw_ref[...]
    xc = cb_ref[...] + cw[3:4, :] * x
    for kk in range(LRU_CONV - 1):
        off = SUBLANES - (LRU_CONV - 1) + kk
        xc = xc + cw[kk:kk + 1, :] * xe[off:off + tt, :]

    xcb = xc.astype(BF16)
    gr = []
    gi = []
    for blk in range(LRU_BLOCKS):
        sl = slice(blk * LRU_BLOCK_DIM, (blk + 1) * LRU_BLOCK_DIM)
        gr.append(_dot(xcb[:, sl], wa_ref[blk]))
        gi.append(_dot(xcb[:, sl], wx_ref[blk]))
    gate_r = jax.nn.sigmoid(jnp.concatenate(gr, axis=1) + ba_ref[...])
    gate_i = jax.nn.sigmoid(jnp.concatenate(gi, axis=1) + bx_ref[...])
    z = -lam_ref[...]
    softplus = jnp.maximum(z, 0.0) + jnp.log1p(jnp.exp(-jnp.abs(z)))
    log_a = (-LRU_C) * gate_r * softplus
    a = jnp.exp(log_a)
    u = jnp.sqrt(1.0 - a * a) * (gate_i * xc)

    nv = tt // SUBLANES
    a3 = a.reshape(nv, SUBLANES, w)
    b3 = u.reshape(nv, SUBLANES, w)
    rix = lax.broadcasted_iota(jnp.int32, (nv, SUBLANES, w), 1)
    s = 1
    while s < SUBLANES:
        keep = rix >= s
        a_sh = jnp.where(keep, pltpu.roll(a3, s, axis=1), 1.0)
        b_sh = jnp.where(keep, pltpu.roll(b3, s, axis=1), 0.0)
        b3 = a3 * b_sh + b3
        a3 = a3 * a_sh
        s *= 2
    carry = carry_ref[...]
    hs = []
    for i in range(nv):
        h_i = a3[i] * carry + b3[i]
        carry = h_i[SUBLANES - 1:SUBLANES, :]
        hs.append(h_i)
    carry_ref[...] = carry
    h = jnp.concatenate(hs, axis=0)
    o_ref[...] = (h * jax.nn.gelu(y_ref[...].astype(F32))).astype(o_ref.dtype)


def rg_lru_block(proj, conv_w, conv_b, wa, ba, wx, bx, lam, layer, batch, seq, col_x):
    t = proj.shape[0]
    w = LRU_WIDTH
    tt = min(256, seq)
    nt = seq // tt
    b0 = col_x // w
    blocks = 2 * 3 * _nbytes((tt, w), BF16) + 24 * _nbytes((tt, w), F32)
    blocks += 4 * _nbytes((LRU_BLOCKS, LRU_BLOCK_DIM, LRU_BLOCK_DIM), BF16)
    kernel = functools.partial(_lru_kernel, tt=tt)
    vec = pl.BlockSpec((1, w), lambda b, n: (0, 0))
    blockdiag = pl.BlockSpec((None, LRU_BLOCKS, LRU_BLOCK_DIM, LRU_BLOCK_DIM),
                             lambda b, n: (layer, 0, 0, 0))
    return pl.pallas_call(
        kernel,
        grid=(batch, nt),
        in_specs=[pl.BlockSpec((tt, w), lambda b, n: (b * nt + n, b0)),
                  pl.BlockSpec((tt, w), lambda b, n: (b * nt + n, b0 + 1)),
                  pl.BlockSpec((LRU_CONV, w), lambda b, n: (0, 0)),
                  vec, blockdiag, vec, blockdiag, vec, vec],
        out_specs=pl.BlockSpec((tt, w), lambda b, n: (b * nt + n, 0)),
        out_shape=jax.ShapeDtypeStruct((t, w), BF16),
        scratch_shapes=[pltpu.VMEM((SUBLANES, w), F32), pltpu.VMEM((1, w), F32)],
        compiler_params=_compiler_params(("arbitrary", "arbitrary"), blocks),
        name="rg_lru",
    )(proj, proj, conv_w, conv_b.reshape(1, w), wa, ba.reshape(1, w), wx, bx.reshape(1, w),
      lam.reshape(1, w))


def _proj_res_norm_kernel(a_ref, w_ref, x_ref, g_ref, xo_ref, ho_ref):
    xn = x_ref[...] + _dot(a_ref[...], w_ref[...])
    xo_ref[...] = xn
    ho_ref[...] = _rms(xn, g_ref[...]).astype(ho_ref.dtype)


def proj_res_norm(a, w, layer, x, g, name):
    t, kdim = a.shape
    d = w.shape[2]
    tm = min(512, t)
    blocks = 2 * (_nbytes((tm, kdim), BF16) + _nbytes((kdim, d), BF16) + 2 * _nbytes((tm, d), F32)
                  + _nbytes((tm, d), BF16)) + 3 * _nbytes((tm, d), F32)
    return pl.pallas_call(
        _proj_res_norm_kernel,
        grid=(t // tm,),
        in_specs=[pl.BlockSpec((tm, kdim), lambda i: (i, 0)),
                  pl.BlockSpec((None, kdim, d), lambda i: (layer, 0, 0)),
                  pl.BlockSpec((tm, d), lambda i: (i, 0)),
                  pl.BlockSpec((1, d), lambda i: (0, 0))],
        out_specs=[pl.BlockSpec((tm, d), lambda i: (i, 0)),
                   pl.BlockSpec((tm, d), lambda i: (i, 0))],
        out_shape=[jax.ShapeDtypeStruct((t, d), F32), jax.ShapeDtypeStruct((t, d), BF16)],
        compiler_params=_compiler_params(("parallel",), blocks),
        name=name,
    )(a, w, x, g.reshape(1, d))


def _xattn_kernel(h_ref, wq_ref, k_ref, v_ref, o_ref, *, head_dim):
    q = _dot(h_ref[...], wq_ref[...]).astype(BF16)
    scale = (head_dim ** -0.5) * LOG2E
    for hd in range(XATTN_HEADS):
        sl = slice(hd * head_dim, (hd + 1) * head_dim)
        s = _dot_nt(q[:, sl], k_ref[:, sl]) * scale
        m = jnp.max(s, axis=-1, keepdims=True)
        e = jnp.exp2(s - m)
        p = (e * (1.0 / jnp.sum(e, axis=-1, keepdims=True))).astype(BF16)
        o_ref[:, sl] = _dot(p, v_ref[:, sl]).astype(o_ref.dtype)


def cross_attention(h, wq, layer, kv, seq, mem_len):
    t, d = h.shape
    tm = min(1024, seq)
    tiles_per_seq = seq // tm
    blocks = 2 * (2 * _nbytes((tm, d), BF16) + 2 * _nbytes((mem_len, d), BF16)) + _nbytes((d, d), BF16)
    blocks += 3 * _nbytes((tm, d), F32)
    kernel = functools.partial(_xattn_kernel, head_dim=d // XATTN_HEADS)
    return pl.pallas_call(
        kernel,
        grid=(t // tm,),
        in_specs=[pl.BlockSpec((tm, d), lambda i: (i, 0)),
                  pl.BlockSpec((None, d, d), lambda i: (layer, 0, 0), pipeline_mode=pl.Buffered(1)),
                  pl.BlockSpec((mem_len, d), lambda i: (i // tiles_per_seq, 0)),
                  pl.BlockSpec((mem_len, d), lambda i: (i // tiles_per_seq, 1))],
        out_specs=pl.BlockSpec((tm, d), lambda i: (i, 0)),
        out_shape=jax.ShapeDtypeStruct((t, d), BF16),
        compiler_params=_compiler_params(("parallel",), blocks),
        name="cross_attention",
    )(h, wq, kv, kv)


def _ffn_kernel(h_ref, halo_ref, wg_ref, wv_ref, cw_ref, cb_ref, wd_ref, x_ref, g_ref, xo_ref, ho_ref,
                hext_ref, acc_ref, *, tm, n_tiles, tiles_per_seq):
    i = pl.program_id(0)
    j = pl.program_id(1)
    halo = BF16_ROWS

    @pl.when(j == 0)
    def _():
        keep = (i % tiles_per_seq) != 0
        hext_ref[0:halo, :] = jnp.where(keep, halo_ref[...], jnp.zeros_like(halo_ref))
        hext_ref[halo:, :] = h_ref[...]
        acc_ref[...] = jnp.zeros_like(acc_ref)

    g = _dot(hext_ref[...], wg_ref[...])
    v = _dot(h_ref[...], wv_ref[...])
    cw = cw_ref[...]
    conv = cb_ref[...] + cw[FFN_CONV - 1:FFN_CONV, :] * g[halo:, :]
    for kk in range(FFN_CONV - 1):
        off = halo - (FFN_CONV - 1) + kk
        conv = conv + cw[kk:kk + 1, :] * g[off:off + tm, :]
    act = (jax.nn.silu(conv) * v).astype(BF16)
    acc_ref[...] += _dot(act, wd_ref[...])

    @pl.when(j == n_tiles - 1)
    def _():
        xn = x_ref[...] + acc_ref[...]
        xo_ref[...] = xn
        ho_ref[...] = _rms(xn, g_ref[...]).astype(ho_ref.dtype)


def ffn(h, w_gate, w_val, conv_w, conv_b, w_down, layer, x, g, seq, h_dtype):
    t, d = h.shape
    n = w_gate.shape[2]
    tn = FFN_TILE_N
    tm = min(512, seq)
    n_tiles = n // tn
    halo = BF16_ROWS
    blocks = 2 * (_nbytes((tm, d), BF16) + 2 * _nbytes((d, tn), BF16) + _nbytes((tn, d), BF16)
                  + 2 * _nbytes((tm, d), F32) + _nbytes((tm, d), h_dtype))
    blocks += _nbytes((tm + halo, d), BF16) + _nbytes((tm, d), F32)
    blocks += 5 * _nbytes((tm, tn), F32) + 2 * _nbytes((tm, d), F32)
    kernel = functools.partial(_ffn_kernel, tm=tm, n_tiles=n_tiles, tiles_per_seq=seq // tm)
    return pl.pallas_call(
        kernel,
        grid=(t // tm, n_tiles),
        in_specs=[pl.BlockSpec((tm, d), lambda i, j: (i, 0)),
                  pl.BlockSpec((halo, d), lambda i, j: (jnp.maximum(i * (tm // halo) - 1, 0), 0)),
                  pl.BlockSpec((None, d, tn), lambda i, j: (layer, 0, j)),
                  pl.BlockSpec((None, d, tn), lambda i, j: (layer, 0, j)),
                  pl.BlockSpec((None, FFN_CONV, tn), lambda i, j: (layer, 0, j)),
                  pl.BlockSpec((None, 1, tn), lambda i, j: (layer, 0, j)),
                  pl.BlockSpec((None, tn, d), lambda i, j: (layer, j, 0)),
                  pl.BlockSpec((tm, d), lambda i, j: (i, 0)),
                  pl.BlockSpec((1, d), lambda i, j: (0, 0))],
        out_specs=[pl.BlockSpec((tm, d), lambda i, j: (i, 0)),
                   pl.BlockSpec((tm, d), lambda i, j: (i, 0))],
        out_shape=[jax.ShapeDtypeStruct((t, d), F32), jax.ShapeDtypeStruct((t, d), h_dtype)],
        scratch_shapes=[pltpu.VMEM((tm + halo, d), BF16), pltpu.VMEM((tm, d), F32)],
        compiler_params=_compiler_params(("parallel", "arbitrary"), blocks),
        name="ffn",
    )(h, h, w_gate, w_val, conv_w, conv_b, w_down, x, g.reshape(1, d))


def _pad_last(a, n):
    return jnp.pad(a, [(0, 0)] * (a.ndim - 1) + [(0, n - a.shape[-1])])


def kernel(x, mem, mem_norm_g, norm_mix_g, w_in, attn_sinks, lru_conv_w, lru_conv_b, lru_wa, lru_ba,
           lru_wx, lru_bx, lru_lambda, w_branch, w_out, norm_xattn_g, xattn_wq, xattn_wkv, xattn_wo,
           norm_ffn_g, ffn_w_up, ffn_conv_w, ffn_conv_b, ffn_w_down, final_norm_g):
    batch, seq, d_model = x.shape
    depth = w_in.shape[0]
    mem_len = mem.shape[1]
    ffn_dim = ffn_w_down.shape[1]
    t = batch * seq
    assert seq % 512 == 0 or seq in (128, 256)
    assert d_model == 2048 and w_in.shape[2] == 13824

    col_ret, col_lru, col_gate = 0, 4 * RET_W, 4 * RET_W + 2 * LRU_WIDTH
    col_q = col_gate + 3 * d_model
    col_k = col_q + SWA_Q_W
    col_v = col_k + SWA_KV_W

    perm_heads = np.asarray(_q_head_order())
    q_cols = (perm_heads[:, None] * SWA_HEAD_DIM + np.arange(SWA_HEAD_DIM)[None, :]).reshape(-1)
    w_in_b = w_in.astype(BF16)
    w_head = jnp.concatenate([w_in[:, :, :SWA_Q_W][:, :, q_cols], w_in[:, :, SWA_Q_W:IN_TILE_N]],
                             axis=2).astype(BF16)
    w_branch_b = w_branch.astype(BF16)
    w_branch_a = w_branch[:, 0][:, q_cols, :].astype(BF16)
    w_out_b = w_out.astype(BF16)
    wq_b = xattn_wq.astype(BF16)
    wkv_b = xattn_wkv.astype(BF16)
    wo_b = xattn_wo.astype(BF16)
    lru_wa_b = lru_wa.astype(BF16)
    lru_wx_b = lru_wx.astype(BF16)
    ffn_pad = -(-ffn_dim // FFN_TILE_N) * FFN_TILE_N
    w_gate = _pad_last(ffn_w_up[:, :, :ffn_dim], ffn_pad).astype(BF16)
    w_val = _pad_last(ffn_w_up[:, :, ffn_dim:], ffn_pad).astype(BF16)
    w_down = jnp.pad(ffn_w_down, ((0, 0), (0, ffn_pad - ffn_dim), (0, 0))).astype(BF16)
    conv_w = _pad_last(ffn_conv_w, ffn_pad)
    conv_b = _pad_last(ffn_conv_b, ffn_pad)[:, None, :]
    cosf, sinf = _rope_tables(seq)

    xf = x.reshape(t, d_model)
    mem_n = rmsnorm(mem.reshape(batch * mem_len, d_model), mem_norm_g)
    h = rmsnorm(xf, norm_mix_g[0])
    for l in range(depth):
        proj = in_proj(h, w_head, w_in_b, l)
        o_a = swa_attention(proj, attn_sinks[l], seq, col_q, col_k, col_v)
        o_c = rg_lru_block(proj, lru_conv_w[l], lru_conv_b[l], lru_wa_b, lru_ba[l], lru_wx_b,
                           lru_bx[l], lru_lambda[l], l, batch, seq, col_lru)
        merged = retention_merge(proj, cosf, sinf, o_a, o_c, w_branch_a, w_branch_b, l, seq, col_ret,
                                 col_gate, d_model)
        xf, h = proj_res_norm(merged, w_out_b, l, xf, norm_xattn_g[l], "mixer_out")

        kv = matmul(mem_n, wkv_b, l, batch * mem_len, 1024, "xattn_kv")
        o_x = cross_attention(h, wq_b, l, kv, seq, mem_len)
        xf, h = proj_res_norm(o_x, wo_b, l, xf, norm_ffn_g[l], "xattn_out")

        last = l == depth - 1
        g_next = final_norm_g if last else norm_mix_g[l + 1]
        xf, h = ffn(h, w_gate, w_val, conv_w, conv_b, w_down, l, xf, g_next, seq,
                    F32 if last else BF16)
    return h.reshape(batch, seq, d_model)
```

```python
import functools

import numpy as np
import jax
import jax.numpy as jnp
from jax import lax
from jax.experimental import pallas as pl
from jax.experimental.pallas import tpu as pltpu

NORM_EPS = 1e-6

LANES = 128
SUBLANES = 8
BF16_ROWS = 16
VMEM_BUDGET_BYTES = 58 * 1024 * 1024

SWA_Q_HEADS = 16
SWA_KV_HEADS = 4
SWA_HEAD_DIM = 64
SWA_BLOCK = 128
SWA_Q_W = SWA_Q_HEADS * SWA_HEAD_DIM
SWA_KV_W = SWA_KV_HEADS * SWA_HEAD_DIM

RET_HEADS = 8
RET_HEAD_DIM = 128
RET_CHUNK = 128
RET_W = RET_HEADS * RET_HEAD_DIM
RET_ROPE_BASE = 10000.0

LRU_WIDTH = 1024
LRU_BLOCKS = 8
LRU_BLOCK_DIM = LRU_WIDTH // LRU_BLOCKS
LRU_CONV = 4
LRU_C = 8.0

XATTN_HEADS = 4

FFN_CONV = 3
FFN_TILE_N = 512
MERGE_CHUNK = 1024
IN_TILE_N = SWA_Q_W + 2 * SWA_KV_W

LOG2E = 1.4426950408889634

BF16 = jnp.bfloat16
F32 = jnp.float32


def _compiler_params(semantics, vmem_bytes):
    return pltpu.CompilerParams(
        dimension_semantics=semantics,
        vmem_limit_bytes=int(min(max(vmem_bytes, 16 * 1024 * 1024), VMEM_BUDGET_BYTES)))


def _nbytes(shape, dtype):
    return int(np.prod(shape)) * jnp.dtype(dtype).itemsize


def _dot(a, b):
    return jnp.dot(a, b, preferred_element_type=F32)


def _dot_nt(a, b):
    return lax.dot_general(a, b, (((1,), (1,)), ((), ())), preferred_element_type=F32)


def _rms(x, g):
    ms = jnp.mean(x * x, axis=-1, keepdims=True)
    return x * lax.rsqrt(ms + NORM_EPS) * g


def _rmsnorm_kernel(x_ref, g_ref, o_ref):
    o_ref[...] = _rms(x_ref[...], g_ref[...]).astype(o_ref.dtype)


def rmsnorm(x, g, out_dtype=BF16):
    t, d = x.shape
    tm = min(512, t)
    blocks = 2 * (_nbytes((tm, d), F32) + _nbytes((tm, d), out_dtype)) + 2 * _nbytes((tm, d), F32)
    return pl.pallas_call(
        _rmsnorm_kernel,
        grid=(t // tm,),
        in_specs=[pl.BlockSpec((tm, d), lambda i: (i, 0)),
                  pl.BlockSpec((1, d), lambda i: (0, 0))],
        out_specs=pl.BlockSpec((tm, d), lambda i: (i, 0)),
        out_shape=jax.ShapeDtypeStruct((t, d), out_dtype),
        compiler_params=_compiler_params(("parallel",), blocks),
        name="rmsnorm",
    )(x, g.reshape(1, d))


def _matmul_kernel(a_ref, w_ref, o_ref):
    o_ref[...] = _dot(a_ref[...], w_ref[...]).astype(o_ref.dtype)


def matmul(a, w, layer, tm, tn, name):
    m, k = a.shape
    n = w.shape[2]
    tm = min(tm, m)
    blocks = (2 * (_nbytes((tm, k), BF16) + _nbytes((k, tn), BF16) + _nbytes((tm, tn), BF16))
              + 2 * _nbytes((tm, tn), F32))
    return pl.pallas_call(
        _matmul_kernel,
        grid=(m // tm, n // tn),
        in_specs=[pl.BlockSpec((tm, k), lambda i, j: (i, 0)),
                  pl.BlockSpec((None, k, tn), lambda i, j: (layer, 0, j))],
        out_specs=pl.BlockSpec((tm, tn), lambda i, j: (i, j)),
        out_shape=jax.ShapeDtypeStruct((m, n), BF16),
        compiler_params=_compiler_params(("parallel", "arbitrary"), blocks),
        name=name,
    )(a, w)


def _in_proj_kernel(h_ref, wh_ref, w_ref, o_ref):
    j = pl.program_id(1)

    @pl.when(j == 0)
    def _():
        o_ref[...] = _dot(h_ref[...], wh_ref[...]).astype(o_ref.dtype)

    @pl.when(j > 0)
    def _():
        o_ref[...] = _dot(h_ref[...], w_ref[...]).astype(o_ref.dtype)


def in_proj(h, w_head, w_in, layer):
    m, k = h.shape
    n = w_in.shape[2]
    tn = IN_TILE_N
    assert w_head.shape[2] == tn
    tm = min(1024, m)
    nt = n // tn
    blocks = (2 * (_nbytes((tm, k), BF16) + 2 * _nbytes((k, tn), BF16) + _nbytes((tm, tn), BF16))
              + 2 * _nbytes((tm, tn), F32))
    return pl.pallas_call(
        _in_proj_kernel,
        grid=(m // tm, nt),
        in_specs=[pl.BlockSpec((tm, k), lambda i, j: (i, 0)),
                  pl.BlockSpec((None, k, tn), lambda i, j: (layer, 0, 0)),
                  pl.BlockSpec((None, k, tn), lambda i, j: (layer, 0, jnp.maximum(j, 1)))],
        out_specs=pl.BlockSpec((tm, tn), lambda i, j: (i, (j + nt - 1) % nt)),
        out_shape=jax.ShapeDtypeStruct((m, n), BF16),
        compiler_params=_compiler_params(("parallel", "arbitrary"), blocks),
        name="in_proj",
    )(h, w_head, w_in)


def _q_head_order():
    order = []
    g = SWA_Q_HEADS // SWA_KV_HEADS
    for kv_pair in range(SWA_KV_HEADS // 2):
        for i in range(g):
            order += [(2 * kv_pair) * g + i, (2 * kv_pair + 1) * g + i]
    return order


def _swa_kernel(sinks_ref, q_ref, kc_ref, vc_ref, kp_ref, vp_ref, o_ref, *, tq, tiles_per_seq):
    i = pl.program_id(0)
    starts_seq = (i % tiles_per_seq) == 0
    blk = SWA_BLOCK
    kwin = jnp.concatenate([kp_ref[...], kc_ref[...]], axis=0)
    vwin = jnp.concatenate([vp_ref[...], vc_ref[...]], axis=0)
    lane = lax.broadcasted_iota(jnp.int32, (blk, LANES), 1)
    lo = lane < SWA_HEAD_DIM
    row = lax.broadcasted_iota(jnp.int32, (2 * blk, 2 * blk), 0)
    col = lax.broadcasted_iota(jnp.int32, (2 * blk, 2 * blk), 1)
    qi = jnp.bitwise_and(row, blk - 1)
    in_window = jnp.logical_and(col > qi, col <= qi + blk)
    in_window_first = jnp.logical_and(in_window, jnp.logical_or(col >= blk, jnp.logical_not(starts_seq)))
    top = lax.broadcasted_iota(jnp.int32, (2 * blk, 1), 0) < blk
    scale = (SWA_HEAD_DIM ** -0.5) * LOG2E
    order = _q_head_order()
    for c in range(tq // blk):
        allowed = in_window_first if c == 0 else in_window
        for p in range(SWA_Q_W // LANES):
            kv_blk = p // (SWA_Q_W // LANES // (SWA_KV_W // LANES))
            kb = kwin[c * blk:(c + 2) * blk, kv_blk * LANES:(kv_blk + 1) * LANES]
            vb = vwin[c * blk:(c + 2) * blk, kv_blk * LANES:(kv_blk + 1) * LANES]
            qb = q_ref[c * blk:(c + 1) * blk, p * LANES:(p + 1) * LANES]
            zero = jnp.zeros_like(qb)
            qs = jnp.concatenate([jnp.where(lo, qb, zero), jnp.where(lo, zero, qb)], axis=0)
            s = _dot_nt(qs, kb) * scale
            s = jnp.where(allowed, s, -jnp.inf)
            sink = jnp.where(top, sinks_ref[order[2 * p]], sinks_ref[order[2 * p + 1]]) * LOG2E
            m = jnp.maximum(jnp.max(s, axis=-1, keepdims=True), sink)
            e = jnp.exp2(s - m)
            denom = jnp.sum(e, axis=-1, keepdims=True) + jnp.exp2(sink - m)
            probs = (e * (1.0 / denom)).astype(BF16)
            o = _dot(probs, vb)
            ob = jnp.where(lo, o[:blk], o[blk:])
            o_ref[c * blk:(c + 1) * blk, p * LANES:(p + 1) * LANES] = ob.astype(o_ref.dtype)


def swa_attention(proj, sinks, seq, col_q, col_k, col_v):
    t = proj.shape[0]
    tq = min(2048, seq)
    nsub = tq // SWA_BLOCK
    qb, kb, vb = col_q // SWA_Q_W, col_k // SWA_KV_W, col_v // SWA_KV_W
    blocks = 2 * (2 * _nbytes((tq, SWA_Q_W), BF16) + 2 * _nbytes((tq + SWA_BLOCK, SWA_KV_W), BF16))
    blocks += 16 * _nbytes((2 * SWA_BLOCK, 2 * SWA_BLOCK), F32)
    kernel = functools.partial(_swa_kernel, tq=tq, tiles_per_seq=seq // tq)
    return pl.pallas_call(
        kernel,
        grid=(t // tq,),
        in_specs=[
            pl.BlockSpec(memory_space=pltpu.SMEM),
            pl.BlockSpec((tq, SWA_Q_W), lambda i: (i, qb)),
            pl.BlockSpec((tq, SWA_KV_W), lambda i: (i, kb)),
            pl.BlockSpec((tq, SWA_KV_W), lambda i: (i, vb)),
            pl.BlockSpec((SWA_BLOCK, SWA_KV_W), lambda i: (jnp.maximum(i * nsub - 1, 0), kb)),
            pl.BlockSpec((SWA_BLOCK, SWA_KV_W), lambda i: (jnp.maximum(i * nsub - 1, 0), vb)),
        ],
        out_specs=pl.BlockSpec((tq, SWA_Q_W), lambda i: (i, 0)),
        out_shape=jax.ShapeDtypeStruct((t, SWA_Q_W), BF16),
        compiler_params=_compiler_params(("parallel",), blocks),
        name="swa_attention",
    )(sinks, proj, proj, proj, proj, proj)


_RET_LOG_GAMMA = [float(v) for v in
                  np.log1p(-np.exp2(-5.0 - np.arange(RET_HEADS, dtype=np.float32))).astype(np.float32)]


def _retention_merge_kernel(q_ref, k_ref, v_ref, g_ref, cos_ref, sin_ref, oa_ref, oc_ref, wa_ref,
                            wb_ref, wc_ref, ga_ref, gb_ref, gc_ref, m_ref, ob_scr, state_ref,
                            decay_ref, *, chunks, tiles_per_seq, n_tiles):
    c = RET_CHUNK
    d = RET_HEAD_DIM
    i = pl.program_id(0)
    slot = i % 2
    tile = jnp.minimum(i, n_tiles - 1)

    @pl.when(i == 0)
    def _():
        ob_scr[...] = jnp.zeros_like(ob_scr)
        qpos = lax.broadcasted_iota(jnp.int32, (c, c), 0)
        kpos = lax.broadcasted_iota(jnp.int32, (c, c), 1)
        diff = (qpos - kpos).astype(F32)
        for h in range(RET_HEADS):
            decay_ref[h] = jnp.where(diff >= 0, jnp.exp(jnp.maximum(diff, 0.0) * _RET_LOG_GAMMA[h]), 0.0)

    @pl.when(tile % tiles_per_seq == 0)
    def _():
        state_ref[...] = jnp.zeros_like(state_ref)

    ob_prev = ob_scr.at[1 - slot]
    ob_cur = ob_scr.at[slot]
    m_chunk = MERGE_CHUNK
    heads_per_chunk = RET_HEADS * m_chunk // m_ref.shape[1]
    idx = lax.broadcasted_iota(jnp.int32, (c, 1), 0).astype(F32)
    for h in range(RET_HEADS):
        if h % heads_per_chunk == 0:
            mc = slice(h // heads_per_chunk * m_chunk, (h // heads_per_chunk + 1) * m_chunk)
            def gated(g_ref, o, w_ref):
                return jax.nn.sigmoid(g_ref[:, mc].astype(F32)) * _dot(o, w_ref[:, mc])

            merged = gated(ga_ref, oa_ref[...], wa_ref)
            merged = merged + gated(gb_ref, ob_prev[...], wb_ref)
            merged = merged + gated(gc_ref, oc_ref[...], wc_ref)
            m_ref[:, mc] = merged.astype(m_ref.dtype)

        lg = _RET_LOG_GAMMA[h]
        sl = slice(h * d, (h + 1) * d)
        q_decay = jnp.exp((idx + 1.0) * lg)
        k_decay = jnp.exp((c - 1.0 - idx) * lg)
        chunk_decay = float(np.exp(np.float32(c) * np.float32(lg)))
        state = state_ref[h]
        for n in range(chunks):
            rows = slice(n * c, (n + 1) * c)
            cosf = cos_ref[rows, :]
            sinf = sin_ref[rows, :]
            q = q_ref[rows, sl].astype(F32)
            k = k_ref[rows, sl].astype(F32)
            v = v_ref[rows, sl]
            qr = q * cosf + pltpu.roll(q, d // 2, axis=1) * sinf
            kr = (k * cosf + pltpu.roll(k, d // 2, axis=1) * sinf) * (d ** -0.5)
            scores = _dot_nt(qr.astype(BF16), kr.astype(BF16)) * decay_ref[h]
            intra = _dot(scores.astype(BF16), v)
            inter = _dot((qr * q_decay).astype(BF16), state.astype(BF16))
            kd_t = (kr * k_decay).T.astype(BF16)
            state = state * chunk_decay + _dot(kd_t, v)
            y = intra + inter
            mu = jnp.mean(y, axis=-1, keepdims=True)
            yc = y - mu
            yn = yc * lax.rsqrt(jnp.mean(yc * yc, axis=-1, keepdims=True) + NORM_EPS)
            gate = g_ref[rows, sl].astype(F32)
            ob_cur[rows, sl] = (jax.nn.silu(gate) * yn).astype(ob_scr.dtype)
        state_ref[h] = state


def _rope_tables(seq):
    half = RET_HEAD_DIM // 2
    theta = RET_ROPE_BASE ** (-jnp.linspace(0.0, 1.0, half, dtype=F32))
    ang = jnp.arange(seq, dtype=F32)[:, None] * theta[None, :]
    cos = jnp.cos(ang)
    sin = jnp.sin(ang)
    return jnp.concatenate([cos, cos], axis=-1), jnp.concatenate([-sin, sin], axis=-1)


def retention_merge(proj, cosf, sinf, o_a, o_c, w_branch_a, w_branch, layer, seq, col_q, col_gate,
                    d_model):
    t = proj.shape[0]
    c = RET_CHUNK
    tr = min(4 * c, seq)
    nt = seq // tr
    n_tiles = t // tr
    b0 = col_q // RET_W
    g0 = col_gate // d_model
    blocks = 2 * (6 * _nbytes((tr, RET_W), BF16) + 4 * _nbytes((tr, d_model), BF16))
    blocks += 4 * _nbytes((tr, RET_HEAD_DIM), F32) + 3 * _nbytes((RET_W, d_model), BF16)
    blocks += 2 * _nbytes((tr, RET_W), BF16) + 4 * _nbytes((tr, MERGE_CHUNK), F32)
    blocks += 2 * _nbytes((RET_HEADS, RET_HEAD_DIM, RET_HEAD_DIM), F32) + 96 * _nbytes((c, c), F32)
    kernel = functools.partial(_retention_merge_kernel, chunks=tr // c, tiles_per_seq=nt,
                               n_tiles=n_tiles)

    def cur(i):
        return jnp.minimum(i, n_tiles - 1)

    def prev(i):
        return jnp.maximum(i - 1, 0)

    def col(j):
        return pl.BlockSpec((tr, RET_W), lambda i: (cur(i), b0 + j))

    def gate(j):
        return pl.BlockSpec((tr, d_model), lambda i: (prev(i), g0 + j))

    def weight(j):
        return pl.BlockSpec((None, None, RET_W, d_model), lambda i: (layer, j, 0, 0),
                            pipeline_mode=pl.Buffered(1))

    o_prev = pl.BlockSpec((tr, RET_W), lambda i: (prev(i), 0))
    return pl.pallas_call(
        kernel,
        grid=(n_tiles + 1,),
        in_specs=[col(0), col(1), col(2), col(3),
                  pl.BlockSpec((tr, RET_HEAD_DIM), lambda i: (cur(i) % nt, 0)),
                  pl.BlockSpec((tr, RET_HEAD_DIM), lambda i: (cur(i) % nt, 0)),
                  o_prev, o_prev,
                  pl.BlockSpec((None, RET_W, d_model), lambda i: (layer, 0, 0),
                               pipeline_mode=pl.Buffered(1)),
                  weight(1), weight(2), gate(0), gate(1), gate(2)],
        out_specs=pl.BlockSpec((tr, d_model), lambda i: (prev(i), 0)),
        out_shape=jax.ShapeDtypeStruct((t, d_model), BF16),
        scratch_shapes=[pltpu.VMEM((2, tr, RET_W), BF16),
                        pltpu.VMEM((RET_HEADS, RET_HEAD_DIM, RET_HEAD_DIM), F32),
                        pltpu.VMEM((RET_HEADS, c, c), F32)],
        compiler_params=_compiler_params(("arbitrary",), blocks),
        name="retention_merge",
    )(proj, proj, proj, proj, cosf, sinf, o_a, o_c, w_branch_a, w_branch, w_branch, proj, proj, proj)


def _lru_kernel(x_ref, y_ref, cw_ref, cb_ref, wa_ref, ba_ref, wx_ref, bx_ref, lam_ref, o_ref,
                xprev_ref, carry_ref, *, tt):
    w = LRU_WIDTH

    @pl.when(pl.program_id(1) == 0)
    def _():
        xprev_ref[...] = jnp.zeros_like(xprev_ref)
        carry_ref[...] = jnp.zeros_like(carry_ref)

    x = x_ref[...].astype(F32)
    xe = jnp.concatenate([xprev_ref[...], x], axis=0)
    xprev_ref[...] = x[tt - SUBLANES:, :]
    cw = cw_ref[...]
    xc = cb_ref[...] + cw[3:4, :] * x
    for kk in range(LRU_CONV - 1):
        off = SUBLANES - (LRU_CONV - 1) + kk
        xc = xc + cw[kk:kk + 1, :] * xe[off:off + tt, :]

    xcb = xc.astype(BF16)
    gr = []
    gi = []
    for blk in range(LRU_BLOCKS):
        sl = slice(blk * LRU_BLOCK_DIM, (blk + 1) * LRU_BLOCK_DIM)
        gr.append(_dot(xcb[:, sl], wa_ref[blk]))
        gi.append(_dot(xcb[:, sl], wx_ref[blk]))
    gate_r = jax.nn.sigmoid(jnp.concatenate(gr, axis=1) + ba_ref[...])
    gate_i = jax.nn.sigmoid(jnp.concatenate(gi, axis=1) + bx_ref[...])
    z = -lam_ref[...]
    softplus = jnp.maximum(z, 0.0) + jnp.log1p(jnp.exp(-jnp.abs(z)))
    log_a = (-LRU_C) * gate_r * softplus
    a = jnp.exp(log_a)
    u = jnp.sqrt(1.0 - a * a) * (gate_i * xc)

    nv = tt // SUBLANES
    a3 = a.reshape(nv, SUBLANES, w)
    b3 = u.reshape(nv, SUBLANES, w)
    rix = lax.broadcasted_iota(jnp.int32, (nv, SUBLANES, w), 1)
    s = 1
    while s < SUBLANES:
        keep = rix >= s
        a_sh = jnp.where(keep, pltpu.roll(a3, s, axis=1), 1.0)
        b_sh = jnp.where(keep, pltpu.roll(b3, s, axis=1), 0.0)
        b3 = a3 * b_sh + b3
        a3 = a3 * a_sh
        s *= 2
    carry = carry_ref[...]
    hs = []
    for i in range(nv):
        h_i = a3[i] * carry + b3[i]
        carry = h_i[SUBLANES - 1:SUBLANES, :]
        hs.append(h_i)
    carry_ref[...] = carry
    h = jnp.concatenate(hs, axis=0)
    o_ref[...] = (h * jax.nn.gelu(y_ref[...].astype(F32))).astype(o_ref.dtype)


def rg_lru_block(proj, conv_w, conv_b, wa, ba, wx, bx, lam, layer, batch, seq, col_x):
    t = proj.shape[0]
    w = LRU_WIDTH
    tt = min(1024, seq)
    nt = seq // tt
    b0 = col_x // w
    blocks = 2 * 3 * _nbytes((tt, w), BF16) + 24 * _nbytes((tt, w), F32)
    blocks += 4 * _nbytes((LRU_BLOCKS, LRU_BLOCK_DIM, LRU_BLOCK_DIM), BF16)
    kernel = functools.partial(_lru_kernel, tt=tt)
    vec = pl.BlockSpec((1, w), lambda b, n: (0, 0))
    blockdiag = pl.BlockSpec((None, LRU_BLOCKS, LRU_BLOCK_DIM, LRU_BLOCK_DIM),
                             lambda b, n: (layer, 0, 0, 0))
    return pl.pallas_call(
        kernel,
        grid=(batch, nt),
        in_specs=[pl.BlockSpec((tt, w), lambda b, n: (b * nt + n, b0)),
                  pl.BlockSpec((tt, w), lambda b, n: (b * nt + n, b0 + 1)),
                  pl.BlockSpec((LRU_CONV, w), lambda b, n: (0, 0)),
                  vec, blockdiag, vec, blockdiag, vec, vec],
        out_specs=pl.BlockSpec((tt, w), lambda b, n: (b * nt + n, 0)),
        out_shape=jax.ShapeDtypeStruct((t, w), BF16),
        scratch_shapes=[pltpu.VMEM((SUBLANES, w), F32), pltpu.VMEM((1, w), F32)],
        compiler_params=_compiler_params(("arbitrary", "arbitrary"), blocks),
        name="rg_lru",
    )(proj, proj, conv_w, conv_b.reshape(1, w), wa, ba.reshape(1, w), wx, bx.reshape(1, w),
      lam.reshape(1, w))


def _proj_res_norm_kernel(a_ref, w_ref, x_ref, g_ref, xo_ref, ho_ref):
    xn = x_ref[...] + _dot(a_ref[...], w_ref[...])
    xo_ref[...] = xn
    ho_ref[...] = _rms(xn, g_ref[...]).astype(ho_ref.dtype)


def proj_res_norm(a, w, layer, x, g, name):
    t, kdim = a.shape
    d = w.shape[2]
    tm = min(512, t)
    blocks = 2 * (_nbytes((tm, kdim), BF16) + _nbytes((kdim, d), BF16) + 2 * _nbytes((tm, d), F32)
                  + _nbytes((tm, d), BF16)) + 3 * _nbytes((tm, d), F32)
    return pl.pallas_call(
        _proj_res_norm_kernel,
        grid=(t // tm,),
        in_specs=[pl.BlockSpec((tm, kdim), lambda i: (i, 0)),
                  pl.BlockSpec((None, kdim, d), lambda i: (layer, 0, 0)),
                  pl.BlockSpec((tm, d), lambda i: (i, 0)),
                  pl.BlockSpec((1, d), lambda i: (0, 0))],
        out_specs=[pl.BlockSpec((tm, d), lambda i: (i, 0)),
                   pl.BlockSpec((tm, d), lambda i: (i, 0))],
        out_shape=[jax.ShapeDtypeStruct((t, d), F32), jax.ShapeDtypeStruct((t, d), BF16)],
        compiler_params=_compiler_params(("parallel",), blocks),
        name=name,
    )(a, w, x, g.reshape(1, d))


def _xattn_kernel(h_ref, wq_ref, k_ref, v_ref, o_ref, *, head_dim):
    q = _dot(h_ref[...], wq_ref[...]).astype(BF16)
    scale = (head_dim ** -0.5) * LOG2E
    for hd in range(XATTN_HEADS):
        sl = slice(hd * head_dim, (hd + 1) * head_dim)
        s = _dot_nt(q[:, sl], k_ref[:, sl]) * scale
        m = jnp.max(s, axis=-1, keepdims=True)
        e = jnp.exp2(s - m)
        p = (e * (1.0 / jnp.sum(e, axis=-1, keepdims=True))).astype(BF16)
        o_ref[:, sl] = _dot(p, v_ref[:, sl]).astype(o_ref.dtype)


def cross_attention(h, wq, layer, kv, seq, mem_len):
    t, d = h.shape
    tm = min(1024, seq)
    tiles_per_seq = seq // tm
    blocks = 2 * (2 * _nbytes((tm, d), BF16) + 2 * _nbytes((mem_len, d), BF16)) + _nbytes((d, d), BF16)
    blocks += 3 * _nbytes((tm, d), F32)
    kernel = functools.partial(_xattn_kernel, head_dim=d // XATTN_HEADS)
    return pl.pallas_call(
        kernel,
        grid=(t // tm,),
        in_specs=[pl.BlockSpec((tm, d), lambda i: (i, 0)),
                  pl.BlockSpec((None, d, d), lambda i: (layer, 0, 0), pipeline_mode=pl.Buffered(1)),
                  pl.BlockSpec((mem_len, d), lambda i: (i // tiles_per_seq, 0)),
                  pl.BlockSpec((mem_len, d), lambda i: (i // tiles_per_seq, 1))],
        out_specs=pl.BlockSpec((tm, d), lambda i: (i, 0)),
        out_shape=jax.ShapeDtypeStruct((t, d), BF16),
        compiler_params=_compiler_params(("parallel",), blocks),
        name="cross_attention",
    )(h, wq, kv, kv)


def _ffn_kernel(h_ref, halo_ref, wg_ref, wv_ref, cw_ref, cb_ref, wd_ref, x_ref, g_ref, xo_ref, ho_ref,
                hext_ref, acc_ref, *, tm, n_tiles, tiles_per_seq):
    i = pl.program_id(0)
    j = pl.program_id(1)
    halo = BF16_ROWS

    @pl.when(j == 0)
    def _():
        keep = (i % tiles_per_seq) != 0
        hext_ref[0:halo, :] = jnp.where(keep, halo_ref[...], jnp.zeros_like(halo_ref))
        hext_ref[halo:, :] = h_ref[...]
        acc_ref[...] = jnp.zeros_like(acc_ref)

    g = _dot(hext_ref[...], wg_ref[...])
    v = _dot(h_ref[...], wv_ref[...])
    cw = cw_ref[...]
    conv = cb_ref[...] + cw[FFN_CONV - 1:FFN_CONV, :] * g[halo:, :]
    for kk in range(FFN_CONV - 1):
        off = halo - (FFN_CONV - 1) + kk
        conv = conv + cw[kk:kk + 1, :] * g[off:off + tm, :]
    act = (jax.nn.silu(conv) * v).astype(BF16)
    acc_ref[...] += _dot(act, wd_ref[...])

    @pl.when(j == n_tiles - 1)
    def _():
        xn = x_ref[...] + acc_ref[...]
        xo_ref[...] = xn
        ho_ref[...] = _rms(xn, g_ref[...]).astype(ho_ref.dtype)


def ffn(h, w_gate, w_val, conv_w, conv_b, w_down, layer, x, g, seq, h_dtype):
    t, d = h.shape
    n = w_gate.shape[2]
    tn = FFN_TILE_N
    tm = min(512, seq)
    n_tiles = n // tn
    halo = BF16_ROWS
    blocks = 2 * (_nbytes((tm, d), BF16) + 2 * _nbytes((d, tn), BF16) + _nbytes((tn, d), BF16)
                  + 2 * _nbytes((tm, d), F32) + _nbytes((tm, d), h_dtype))
    blocks += _nbytes((tm + halo, d), BF16) + _nbytes((tm, d), F32)
    blocks += 5 * _nbytes((tm, tn), F32) + 2 * _nbytes((tm, d), F32)
    kernel = functools.partial(_ffn_kernel, tm=tm, n_tiles=n_tiles, tiles_per_seq=seq // tm)

    def x_tile(i, j):
        return jnp.where(j >= n_tiles // 2, i, jnp.maximum(i - 1, 0))

    return pl.pallas_call(
        kernel,
        grid=(t // tm, n_tiles),
        in_specs=[pl.BlockSpec((tm, d), lambda i, j: (i, 0)),
                  pl.BlockSpec((halo, d), lambda i, j: (jnp.maximum(i * (tm // halo) - 1, 0), 0)),
                  pl.BlockSpec((None, d, tn), lambda i, j: (layer, 0, j)),
                  pl.BlockSpec((None, d, tn), lambda i, j: (layer, 0, j)),
                  pl.BlockSpec((None, FFN_CONV, tn), lambda i, j: (layer, 0, j)),
                  pl.BlockSpec((None, 1, tn), lambda i, j: (layer, 0, j)),
                  pl.BlockSpec((None, tn, d), lambda i, j: (layer, j, 0)),
                  pl.BlockSpec((tm, d), lambda i, j: (x_tile(i, j), 0)),
                  pl.BlockSpec((1, d), lambda i, j: (0, 0))],
        out_specs=[pl.BlockSpec((tm, d), lambda i, j: (i, 0)),
                   pl.BlockSpec((tm, d), lambda i, j: (i, 0))],
        out_shape=[jax.ShapeDtypeStruct((t, d), F32), jax.ShapeDtypeStruct((t, d), h_dtype)],
        scratch_shapes=[pltpu.VMEM((tm + halo, d), BF16), pltpu.VMEM((tm, d), F32)],
        compiler_params=_compiler_params(("parallel", "arbitrary"), blocks),
        name="ffn",
    )(h, h, w_gate, w_val, conv_w, conv_b, w_down, x, g.reshape(1, d))


def _pad_last(a, n):
    return jnp.pad(a, [(0, 0)] * (a.ndim - 1) + [(0, n - a.shape[-1])])


def kernel(x, mem, mem_norm_g, norm_mix_g, w_in, attn_sinks, lru_conv_w, lru_conv_b, lru_wa, lru_ba,
           lru_wx, lru_bx, lru_lambda, w_branch, w_out, norm_xattn_g, xattn_wq, xattn_wkv, xattn_wo,
           norm_ffn_g, ffn_w_up, ffn_conv_w, ffn_conv_b, ffn_w_down, final_norm_g):
    batch, seq, d_model = x.shape
    depth = w_in.shape[0]
    mem_len = mem.shape[1]
    ffn_dim = ffn_w_down.shape[1]
    t = batch * seq
    assert seq % 512 == 0 or seq in (128, 256)
    assert d_model == 2048 and w_in.shape[2] == 13824

    col_ret, col_lru, col_gate = 0, 4 * RET_W, 4 * RET_W + 2 * LRU_WIDTH
    col_q = col_gate + 3 * d_model
    col_k = col_q + SWA_Q_W
    col_v = col_k + SWA_KV_W

    perm_heads = np.asarray(_q_head_order())
    q_cols = (perm_heads[:, None] * SWA_HEAD_DIM + np.arange(SWA_HEAD_DIM)[None, :]).reshape(-1)
    w_in_b = w_in.astype(BF16)
    w_head = jnp.concatenate([w_in[:, :, :SWA_Q_W][:, :, q_cols], w_in[:, :, SWA_Q_W:IN_TILE_N]],
                             axis=2).astype(BF16)
    w_branch_b = w_branch.astype(BF16)
    w_branch_a = w_branch[:, 0][:, q_cols, :].astype(BF16)
    w_out_b = w_out.astype(BF16)
    wq_b = xattn_wq.astype(BF16)
    wkv_b = xattn_wkv.astype(BF16)
    wo_b = xattn_wo.astype(BF16)
    lru_wa_b = lru_wa.astype(BF16)
    lru_wx_b = lru_wx.astype(BF16)
    ffn_pad = -(-ffn_dim // FFN_TILE_N) * FFN_TILE_N
    w_gate = _pad_last(ffn_w_up[:, :, :ffn_dim], ffn_pad).astype(BF16)
    w_val = _pad_last(ffn_w_up[:, :, ffn_dim:], ffn_pad).astype(BF16)
    w_down = jnp.pad(ffn_w_down, ((0, 0), (0, ffn_pad - ffn_dim), (0, 0))).astype(BF16)
    conv_w = _pad_last(ffn_conv_w, ffn_pad)
    conv_b = _pad_last(ffn_conv_b, ffn_pad)[:, None, :]
    cosf, sinf = _rope_tables(seq)

    xf = x.reshape(t, d_model)
    mem_n = rmsnorm(mem.reshape(batch * mem_len, d_model), mem_norm_g)
    h = rmsnorm(xf, norm_mix_g[0])
    for l in range(depth):
        proj = in_proj(h, w_head, w_in_b, l)
        o_a = swa_attention(proj, attn_sinks[l], seq, col_q, col_k, col_v)
        o_c = rg_lru_block(proj, lru_conv_w[l], lru_conv_b[l], lru_wa_b, lru_ba[l], lru_wx_b,
                           lru_bx[l], lru_lambda[l], l, batch, seq, col_lru)
        merged = retention_merge(proj, cosf, sinf, o_a, o_c, w_branch_a, w_branch_b, l, seq, col_ret,
                                 col_gate, d_model)
        xf, h = proj_res_norm(merged, w_out_b, l, xf, norm_xattn_g[l], "mixer_out")

        kv = matmul(mem_n, wkv_b, l, batch * mem_len, 1024, "xattn_kv")
        o_x = cross_attention(h, wq_b, l, kv, seq, mem_len)
        xf, h = proj_res_norm(o_x, wo_b, l, xf, norm_ffn_g[l], "xattn_out")

        last = l == depth - 1
        g_next = final_norm_g if last else norm_mix_g[l + 1]
        xf, h = ffn(h, w_gate, w_val, conv_w, conv_b, w_down, l, xf, g_next, seq,
                    F32 if last else BF16)
    return h.reshape(batch, seq, d_model)
```

```python
import functools

import numpy as np
import jax
import jax.numpy as jnp
from jax import lax
from jax.experimental import pallas as pl
from jax.experimental.pallas import tpu as pltpu

NORM_EPS = 1e-6

LANES = 128
SUBLANES = 8
BF16_ROWS = 16
VMEM_BUDGET_BYTES = 58 * 1024 * 1024

SWA_Q_HEADS = 16
SWA_KV_HEADS = 4
SWA_HEAD_DIM = 64
SWA_BLOCK = 128
SWA_Q_W = SWA_Q_HEADS * SWA_HEAD_DIM
SWA_KV_W = SWA_KV_HEADS * SWA_HEAD_DIM

RET_HEADS = 8
RET_HEAD_DIM = 128
RET_CHUNK = 128
RET_W = RET_HEADS * RET_HEAD_DIM
RET_ROPE_BASE = 10000.0

LRU_WIDTH = 1024
LRU_BLOCKS = 8
LRU_BLOCK_DIM = LRU_WIDTH // LRU_BLOCKS
LRU_CONV = 4
LRU_C = 8.0

XATTN_HEADS = 4

FFN_CONV = 3
FFN_TILE_N = 512
MERGE_CHUNK = 1024
IN_TILE_N = SWA_Q_W + 2 * SWA_KV_W

LOG2E = 1.4426950408889634

BF16 = jnp.bfloat16
F32 = jnp.float32


def _compiler_params(semantics, vmem_bytes):
    return pltpu.CompilerParams(
        dimension_semantics=semantics,
        vmem_limit_bytes=int(min(max(vmem_bytes, 16 * 1024 * 1024), VMEM_BUDGET_BYTES)))


def _nbytes(shape, dtype):
    return int(np.prod(shape)) * jnp.dtype(dtype).itemsize


def _dot(a, b):
    return jnp.dot(a, b, preferred_element_type=F32)


def _dot_nt(a, b):
    return lax.dot_general(a, b, (((1,), (1,)), ((), ())), preferred_element_type=F32)


def _rms(x, g):
    ms = jnp.mean(x * x, axis=-1, keepdims=True)
    return x * lax.rsqrt(ms + NORM_EPS) * g


def _rmsnorm_kernel(x_ref, g_ref, o_ref):
    o_ref[...] = _rms(x_ref[...], g_ref[...]).astype(o_ref.dtype)


def rmsnorm(x, g, out_dtype=BF16):
    t, d = x.shape
    tm = min(512, t)
    blocks = 2 * (_nbytes((tm, d), F32) + _nbytes((tm, d), out_dtype)) + 2 * _nbytes((tm, d), F32)
    return pl.pallas_call(
        _rmsnorm_kernel,
        grid=(t // tm,),
        in_specs=[pl.BlockSpec((tm, d), lambda i: (i, 0)),
                  pl.BlockSpec((1, d), lambda i: (0, 0))],
        out_specs=pl.BlockSpec((tm, d), lambda i: (i, 0)),
        out_shape=jax.ShapeDtypeStruct((t, d), out_dtype),
        compiler_params=_compiler_params(("parallel",), blocks),
        name="rmsnorm",
    )(x, g.reshape(1, d))


def _matmul_kernel(a_ref, w_ref, o_ref):
    o_ref[...] = _dot(a_ref[...], w_ref[...]).astype(o_ref.dtype)


def matmul(a, w, layer, tm, tn, name):
    m, k = a.shape
    n = w.shape[2]
    tm = min(tm, m)
    blocks = (2 * (_nbytes((tm, k), BF16) + _nbytes((k, tn), BF16) + _nbytes((tm, tn), BF16))
              + 2 * _nbytes((tm, tn), F32))
    return pl.pallas_call(
        _matmul_kernel,
        grid=(m // tm, n // tn),
        in_specs=[pl.BlockSpec((tm, k), lambda i, j: (i, 0)),
                  pl.BlockSpec((None, k, tn), lambda i, j: (layer, 0, j))],
        out_specs=pl.BlockSpec((tm, tn), lambda i, j: (i, j)),
        out_shape=jax.ShapeDtypeStruct((m, n), BF16),
        compiler_params=_compiler_params(("parallel", "arbitrary"), blocks),
        name=name,
    )(a, w)


def _in_proj_kernel(h_ref, wh_ref, w_ref, o_ref):
    j = pl.program_id(1)

    @pl.when(j == 0)
    def _():
        o_ref[...] = _dot(h_ref[...], wh_ref[...]).astype(o_ref.dtype)

    @pl.when(j > 0)
    def _():
        o_ref[...] = _dot(h_ref[...], w_ref[...]).astype(o_ref.dtype)


def in_proj(h, w_head, w_in, layer):
    m, k = h.shape
    n = w_in.shape[2]
    tn = IN_TILE_N
    assert w_head.shape[2] == tn
    tm = min(1024, m)
    nt = n // tn
    blocks = (2 * (_nbytes((tm, k), BF16) + 2 * _nbytes((k, tn), BF16) + _nbytes((tm, tn), BF16))
              + 2 * _nbytes((tm, tn), F32))
    return pl.pallas_call(
        _in_proj_kernel,
        grid=(m // tm, nt),
        in_specs=[pl.BlockSpec((tm, k), lambda i, j: (i, 0)),
                  pl.BlockSpec((None, k, tn), lambda i, j: (layer, 0, 0)),
                  pl.BlockSpec((None, k, tn), lambda i, j: (layer, 0, jnp.maximum(j, 1)))],
        out_specs=pl.BlockSpec((tm, tn), lambda i, j: (i, (j + nt - 1) % nt)),
        out_shape=jax.ShapeDtypeStruct((m, n), BF16),
        compiler_params=_compiler_params(("parallel", "arbitrary"), blocks),
        name="in_proj",
    )(h, w_head, w_in)


def _q_head_order():
    order = []
    g = SWA_Q_HEADS // SWA_KV_HEADS
    for kv_pair in range(SWA_KV_HEADS // 2):
        for i in range(g):
            order += [(2 * kv_pair) * g + i, (2 * kv_pair + 1) * g + i]
    return order


def _swa_kernel(sinks_ref, q_ref, kc_ref, vc_ref, kp_ref, vp_ref, o_ref, *, tq, tiles_per_seq):
    i = pl.program_id(0)
    starts_seq = (i % tiles_per_seq) == 0
    blk = SWA_BLOCK
    kwin = jnp.concatenate([kp_ref[...], kc_ref[...]], axis=0)
    vwin = jnp.concatenate([vp_ref[...], vc_ref[...]], axis=0)
    lane = lax.broadcasted_iota(jnp.int32, (blk, LANES), 1)
    lo = lane < SWA_HEAD_DIM
    row = lax.broadcasted_iota(jnp.int32, (2 * blk, 2 * blk), 0)
    col = lax.broadcasted_iota(jnp.int32, (2 * blk, 2 * blk), 1)
    qi = jnp.bitwise_and(row, blk - 1)
    in_window = jnp.logical_and(col > qi, col <= qi + blk)
    in_window_first = jnp.logical_and(in_window, jnp.logical_or(col >= blk, jnp.logical_not(starts_seq)))
    top = lax.broadcasted_iota(jnp.int32, (2 * blk, 1), 0) < blk
    scale = (SWA_HEAD_DIM ** -0.5) * LOG2E
    order = _q_head_order()
    for c in range(tq // blk):
        allowed = in_window_first if c == 0 else in_window
        for p in range(SWA_Q_W // LANES):
            kv_blk = p // (SWA_Q_W // LANES // (SWA_KV_W // LANES))
            kb = kwin[c * blk:(c + 2) * blk, kv_blk * LANES:(kv_blk + 1) * LANES]
            vb = vwin[c * blk:(c + 2) * blk, kv_blk * LANES:(kv_blk + 1) * LANES]
            qb = q_ref[c * blk:(c + 1) * blk, p * LANES:(p + 1) * LANES]
            zero = jnp.zeros_like(qb)
            qs = jnp.concatenate([jnp.where(lo, qb, zero), jnp.where(lo, zero, qb)], axis=0)
            s = _dot_nt(qs, kb) * scale
            s = jnp.where(allowed, s, -jnp.inf)
            sink = jnp.where(top, sinks_ref[order[2 * p]], sinks_ref[order[2 * p + 1]]) * LOG2E
            m = jnp.maximum(jnp.max(s, axis=-1, keepdims=True), sink)
            e = jnp.exp2(s - m)
            denom = jnp.sum(e, axis=-1, keepdims=True) + jnp.exp2(sink - m)
            probs = (e * (1.0 / denom)).astype(BF16)
            o = _dot(probs, vb)
            ob = jnp.where(lo, o[:blk], o[blk:])
            o_ref[c * blk:(c + 1) * blk, p * LANES:(p + 1) * LANES] = ob.astype(o_ref.dtype)


def swa_attention(proj, sinks, seq, col_q, col_k, col_v):
    t = proj.shape[0]
    tq = min(2048, seq)
    nsub = tq // SWA_BLOCK
    qb, kb, vb = col_q // SWA_Q_W, col_k // SWA_KV_W, col_v // SWA_KV_W
    blocks = 2 * (2 * _nbytes((tq, SWA_Q_W), BF16) + 2 * _nbytes((tq + SWA_BLOCK, SWA_KV_W), BF16))
    blocks += 16 * _nbytes((2 * SWA_BLOCK, 2 * SWA_BLOCK), F32)
    kernel = functools.partial(_swa_kernel, tq=tq, tiles_per_seq=seq // tq)
    return pl.pallas_call(
        kernel,
        grid=(t // tq,),
        in_specs=[
            pl.BlockSpec(memory_space=pltpu.SMEM),
            pl.BlockSpec((tq, SWA_Q_W), lambda i: (i, qb)),
            pl.BlockSpec((tq, SWA_KV_W), lambda i: (i, kb)),
            pl.BlockSpec((tq, SWA_KV_W), lambda i: (i, vb)),
            pl.BlockSpec((SWA_BLOCK, SWA_KV_W), lambda i: (jnp.maximum(i * nsub - 1, 0), kb)),
            pl.BlockSpec((SWA_BLOCK, SWA_KV_W), lambda i: (jnp.maximum(i * nsub - 1, 0), vb)),
        ],
        out_specs=pl.BlockSpec((tq, SWA_Q_W), lambda i: (i, 0)),
        out_shape=jax.ShapeDtypeStruct((t, SWA_Q_W), BF16),
        compiler_params=_compiler_params(("parallel",), blocks),
        name="swa_attention",
    )(sinks, proj, proj, proj, proj, proj)


_RET_LOG_GAMMA = [float(v) for v in
                  np.log1p(-np.exp2(-5.0 - np.arange(RET_HEADS, dtype=np.float32))).astype(np.float32)]


def _retention_merge_kernel(q_ref, k_ref, v_ref, g_ref, cos_ref, sin_ref, oa_ref, oc_ref, wa_ref,
                            wb_ref, wc_ref, ga_ref, gb_ref, gc_ref, m_ref, ob_scr, state_ref,
                            decay_ref, *, chunks, tiles_per_seq, n_tiles):
    c = RET_CHUNK
    d = RET_HEAD_DIM
    i = pl.program_id(0)
    slot = i % 2
    tile = jnp.minimum(i, n_tiles - 1)

    @pl.when(i == 0)
    def _():
        ob_scr[...] = jnp.zeros_like(ob_scr)
        qpos = lax.broadcasted_iota(jnp.int32, (c, c), 0)
        kpos = lax.broadcasted_iota(jnp.int32, (c, c), 1)
        diff = (qpos - kpos).astype(F32)
        for h in range(RET_HEADS):
            decay_ref[h] = jnp.where(diff >= 0, jnp.exp(jnp.maximum(diff, 0.0) * _RET_LOG_GAMMA[h]), 0.0)

    @pl.when(tile % tiles_per_seq == 0)
    def _():
        state_ref[...] = jnp.zeros_like(state_ref)

    ob_prev = ob_scr.at[1 - slot]
    ob_cur = ob_scr.at[slot]
    m_chunk = MERGE_CHUNK
    heads_per_chunk = RET_HEADS * m_chunk // m_ref.shape[1]
    idx = lax.broadcasted_iota(jnp.int32, (c, 1), 0).astype(F32)
    for h in range(RET_HEADS):
        if h % heads_per_chunk == 0:
            mc = slice(h // heads_per_chunk * m_chunk, (h // heads_per_chunk + 1) * m_chunk)
            def gated(g_ref, o, w_ref):
                return jax.nn.sigmoid(g_ref[:, mc].astype(F32)) * _dot(o, w_ref[:, mc])

            merged = gated(ga_ref, oa_ref[...], wa_ref)
            merged = merged + gated(gb_ref, ob_prev[...], wb_ref)
            merged = merged + gated(gc_ref, oc_ref[...], wc_ref)
            m_ref[:, mc] = merged.astype(m_ref.dtype)

        lg = _RET_LOG_GAMMA[h]
        sl = slice(h * d, (h + 1) * d)
        q_decay = jnp.exp((idx + 1.0) * lg)
        k_decay = jnp.exp((c - 1.0 - idx) * lg)
        chunk_decay = float(np.exp(np.float32(c) * np.float32(lg)))
        state = state_ref[h]
        for n in range(chunks):
            rows = slice(n * c, (n + 1) * c)
            cosf = cos_ref[rows, :]
            sinf = sin_ref[rows, :]
            q = q_ref[rows, sl].astype(F32)
            k = k_ref[rows, sl].astype(F32)
            v = v_ref[rows, sl]
            qr = q * cosf + pltpu.roll(q, d // 2, axis=1) * sinf
            kr = (k * cosf + pltpu.roll(k, d // 2, axis=1) * sinf) * (d ** -0.5)
            scores = _dot_nt(qr.astype(BF16), kr.astype(BF16)) * decay_ref[h]
            intra = _dot(scores.astype(BF16), v)
            inter = _dot((qr * q_decay).astype(BF16), state.astype(BF16))
            kd_t = (kr * k_decay).T.astype(BF16)
            state = state * chunk_decay + _dot(kd_t, v)
            y = intra + inter
            mu = jnp.mean(y, axis=-1, keepdims=True)
            yc = y - mu
            yn = yc * lax.rsqrt(jnp.mean(yc * yc, axis=-1, keepdims=True) + NORM_EPS)
            gate = g_ref[rows, sl].astype(F32)
            ob_cur[rows, sl] = (jax.nn.silu(gate) * yn).astype(ob_scr.dtype)
        state_ref[h] = state


def _rope_tables(seq):
    half = RET_HEAD_DIM // 2
    theta = RET_ROPE_BASE ** (-jnp.linspace(0.0, 1.0, half, dtype=F32))
    ang = jnp.arange(seq, dtype=F32)[:, None] * theta[None, :]
    cos = jnp.cos(ang)
    sin = jnp.sin(ang)
    return jnp.concatenate([cos, cos], axis=-1), jnp.concatenate([-sin, sin], axis=-1)


def retention_merge(proj, cosf, sinf, o_a, o_c, w_branch_a, w_branch, layer, seq, col_q, col_gate,
                    d_model):
    t = proj.shape[0]
    c = RET_CHUNK
    tr = min(4 * c, seq)
    nt = seq // tr
    n_tiles = t // tr
    b0 = col_q // RET_W
    g0 = col_gate // d_model
    blocks = 2 * (6 * _nbytes((tr, RET_W), BF16) + 4 * _nbytes((tr, d_model), BF16))
    blocks += 4 * _nbytes((tr, RET_HEAD_DIM), F32) + 3 * _nbytes((RET_W, d_model), BF16)
    blocks += 2 * _nbytes((tr, RET_W), BF16) + 4 * _nbytes((tr, MERGE_CHUNK), F32)
    blocks += 2 * _nbytes((RET_HEADS, RET_HEAD_DIM, RET_HEAD_DIM), F32) + 96 * _nbytes((c, c), F32)
    kernel = functools.partial(_retention_merge_kernel, chunks=tr // c, tiles_per_seq=nt,
                               n_tiles=n_tiles)

    def cur(i):
        return jnp.minimum(i, n_tiles - 1)

    def prev(i):
        return jnp.maximum(i - 1, 0)

    def col(j):
        return pl.BlockSpec((tr, RET_W), lambda i: (cur(i), b0 + j))

    def gate(j):
        return pl.BlockSpec((tr, d_model), lambda i: (prev(i), g0 + j))

    def weight(j):
        return pl.BlockSpec((None, None, RET_W, d_model), lambda i: (layer, j, 0, 0),
                            pipeline_mode=pl.Buffered(1))

    o_prev = pl.BlockSpec((tr, RET_W), lambda i: (prev(i), 0))
    return pl.pallas_call(
        kernel,
        grid=(n_tiles + 1,),
        in_specs=[col(0), col(1), col(2), col(3),
                  pl.BlockSpec((tr, RET_HEAD_DIM), lambda i: (cur(i) % nt, 0)),
                  pl.BlockSpec((tr, RET_HEAD_DIM), lambda i: (cur(i) % nt, 0)),
                  o_prev, o_prev,
                  pl.BlockSpec((None, RET_W, d_model), lambda i: (layer, 0, 0),
                               pipeline_mode=pl.Buffered(1)),
                  weight(1), weight(2), gate(0), gate(1), gate(2)],
        out_specs=pl.BlockSpec((tr, d_model), lambda i: (prev(i), 0)),
        out_shape=jax.ShapeDtypeStruct((t, d_model), BF16),
        scratch_shapes=[pltpu.VMEM((2, tr, RET_W), BF16),
                        pltpu.VMEM((RET_HEADS, RET_HEAD_DIM, RET_HEAD_DIM), F32),
                        pltpu.VMEM((RET_HEADS, c, c), F32)],
        compiler_params=_compiler_params(("arbitrary",), blocks),
        name="retention_merge",
    )(proj, proj, proj, proj, cosf, sinf, o_a, o_c, w_branch_a, w_branch, w_branch, proj, proj, proj)


def _lru_kernel(x_ref, y_ref, cw_ref, cb_ref, wa_ref, ba_ref, wx_ref, bx_ref, lam_ref, o_ref,
                xprev_ref, carry_ref, *, tt):
    w = LRU_WIDTH

    @pl.when(pl.program_id(1) == 0)
    def _():
        xprev_ref[...] = jnp.zeros_like(xprev_ref)
        carry_ref[...] = jnp.zeros_like(carry_ref)

    x = x_ref[...].astype(F32)
    xe = jnp.concatenate([xprev_ref[...], x], axis=0)
    xprev_ref[...] = x[tt - SUBLANES:, :]
    cw = cw_ref[...]
    xc = cb_ref[...] + cw[3:4, :] * x
    for kk in range(LRU_CONV - 1):
        off = SUBLANES - (LRU_CONV - 1) + kk
        xc = xc + cw[kk:kk + 1, :] * xe[off:off + tt, :]

    xcb = xc.astype(BF16)
    gr = []
    gi = []
    for blk in range(LRU_BLOCKS):
        sl = slice(blk * LRU_BLOCK_DIM, (blk + 1) * LRU_BLOCK_DIM)
        gr.append(_dot(xcb[:, sl], wa_ref[blk]))
        gi.append(_dot(xcb[:, sl], wx_ref[blk]))
    gate_r = jax.nn.sigmoid(jnp.concatenate(gr, axis=1) + ba_ref[...])
    gate_i = jax.nn.sigmoid(jnp.concatenate(gi, axis=1) + bx_ref[...])
    z = -lam_ref[...]
    softplus = jnp.maximum(z, 0.0) + jnp.log1p(jnp.exp(-jnp.abs(z)))
    log_a = (-LRU_C) * gate_r * softplus
    a = jnp.exp(log_a)
    u = jnp.sqrt(1.0 - a * a) * (gate_i * xc)

    nv = tt // SUBLANES
    a3 = a.reshape(nv, SUBLANES, w)
    b3 = u.reshape(nv, SUBLANES, w)
    rix = lax.broadcasted_iota(jnp.int32, (nv, SUBLANES, w), 1)
    s = 1
    while s < SUBLANES:
        keep = rix >= s
        a_sh = jnp.where(keep, pltpu.roll(a3, s, axis=1), 1.0)
        b_sh = jnp.where(keep, pltpu.roll(b3, s, axis=1), 0.0)
        b3 = a3 * b_sh + b3
        a3 = a3 * a_sh
        s *= 2
    carry = carry_ref[...]
    hs = []
    for i in range(nv):
        h_i = a3[i] * carry + b3[i]
        carry = h_i[SUBLANES - 1:SUBLANES, :]
        hs.append(h_i)
    carry_ref[...] = carry
    h = jnp.concatenate(hs, axis=0)
    o_ref[...] = (h * jax.nn.gelu(y_ref[...].astype(F32))).astype(o_ref.dtype)


def rg_lru_block(proj, conv_w, conv_b, wa, ba, wx, bx, lam, layer, batch, seq, col_x):
    t = proj.shape[0]
    w = LRU_WIDTH
    tt = min(1024, seq)
    nt = seq // tt
    b0 = col_x // w
    blocks = 2 * 3 * _nbytes((tt, w), BF16) + 24 * _nbytes((tt, w), F32)
    blocks += 4 * _nbytes((LRU_BLOCKS, LRU_BLOCK_DIM, LRU_BLOCK_DIM), BF16)
    kernel = functools.partial(_lru_kernel, tt=tt)
    vec = pl.BlockSpec((1, w), lambda b, n: (0, 0))
    blockdiag = pl.BlockSpec((None, LRU_BLOCKS, LRU_BLOCK_DIM, LRU_BLOCK_DIM),
                             lambda b, n: (layer, 0, 0, 0))
    return pl.pallas_call(
        kernel,
        grid=(batch, nt),
        in_specs=[pl.BlockSpec((tt, w), lambda b, n: (b * nt + n, b0)),
                  pl.BlockSpec((tt, w), lambda b, n: (b * nt + n, b0 + 1)),
                  pl.BlockSpec((LRU_CONV, w), lambda b, n: (0, 0)),
                  vec, blockdiag, vec, blockdiag, vec, vec],
        out_specs=pl.BlockSpec((tt, w), lambda b, n: (b * nt + n, 0)),
        out_shape=jax.ShapeDtypeStruct((t, w), BF16),
        scratch_shapes=[pltpu.VMEM((SUBLANES, w), F32), pltpu.VMEM((1, w), F32)],
        compiler_params=_compiler_params(("arbitrary", "arbitrary"), blocks),
        name="rg_lru",
    )(proj, proj, conv_w, conv_b.reshape(1, w), wa, ba.reshape(1, w), wx, bx.reshape(1, w),
      lam.reshape(1, w))


def _proj_res_norm_kernel(a_ref, w_ref, x_ref, g_ref, xo_ref, ho_ref):
    xn = x_ref[...] + _dot(a_ref[...], w_ref[...])
    xo_ref[...] = xn
    ho_ref[...] = _rms(xn, g_ref[...]).astype(ho_ref.dtype)


def proj_res_norm(a, w, layer, x, g, name):
    t, kdim = a.shape
    d = w.shape[2]
    tm = min(512, t)
    blocks = 2 * (_nbytes((tm, kdim), BF16) + _nbytes((kdim, d), BF16) + 2 * _nbytes((tm, d), F32)
                  + _nbytes((tm, d), BF16)) + 3 * _nbytes((tm, d), F32)
    return pl.pallas_call(
        _proj_res_norm_kernel,
        grid=(t // tm,),
        in_specs=[pl.BlockSpec((tm, kdim), lambda i: (i, 0)),
                  pl.BlockSpec((None, kdim, d), lambda i: (layer, 0, 0)),
                  pl.BlockSpec((tm, d), lambda i: (i, 0)),
                  pl.BlockSpec((1, d), lambda i: (0, 0))],
        out_specs=[pl.BlockSpec((tm, d), lambda i: (i, 0)),
                   pl.BlockSpec((tm, d), lambda i: (i, 0))],
        out_shape=[jax.ShapeDtypeStruct((t, d), F32), jax.ShapeDtypeStruct((t, d), BF16)],
        compiler_params=_compiler_params(("parallel",), blocks),
        name=name,
    )(a, w, x, g.reshape(1, d))


def _xattn_kernel(h_ref, wq_ref, k_ref, v_ref, o_ref, *, head_dim):
    q = _dot(h_ref[...], wq_ref[...]).astype(BF16)
    scale = (head_dim ** -0.5) * LOG2E
    for hd in range(XATTN_HEADS):
        sl = slice(hd * head_dim, (hd + 1) * head_dim)
        s = _dot_nt(q[:, sl], k_ref[:, sl]) * scale
        m = jnp.max(s, axis=-1, keepdims=True)
        e = jnp.exp2(s - m)
        p = (e * (1.0 / jnp.sum(e, axis=-1, keepdims=True))).astype(BF16)
        o_ref[:, sl] = _dot(p, v_ref[:, sl]).astype(o_ref.dtype)


def cross_attention(h, wq, layer, kv, seq, mem_len):
    t, d = h.shape
    tm = min(1024, seq)
    tiles_per_seq = seq // tm
    blocks = 2 * (2 * _nbytes((tm, d), BF16) + 2 * _nbytes((mem_len, d), BF16)) + _nbytes((d, d), BF16)
    blocks += 3 * _nbytes((tm, d), F32)
    kernel = functools.partial(_xattn_kernel, head_dim=d // XATTN_HEADS)
    return pl.pallas_call(
        kernel,
        grid=(t // tm,),
        in_specs=[pl.BlockSpec((tm, d), lambda i: (i, 0)),
                  pl.BlockSpec((None, d, d), lambda i: (layer, 0, 0), pipeline_mode=pl.Buffered(1)),
                  pl.BlockSpec((mem_len, d), lambda i: (i // tiles_per_seq, 0)),
                  pl.BlockSpec((mem_len, d), lambda i: (i // tiles_per_seq, 1))],
        out_specs=pl.BlockSpec((tm, d), lambda i: (i, 0)),
        out_shape=jax.ShapeDtypeStruct((t, d), BF16),
        compiler_params=_compiler_params(("parallel",), blocks),
        name="cross_attention",
    )(h, wq, kv, kv)


def _ffn_kernel(h_ref, halo_ref, wg_ref, wv_ref, cw_ref, cb_ref, wd_ref, x_ref, g_ref, xo_ref, ho_ref,
                hext_ref, acc_ref, *, tm, n_tiles, tiles_per_seq):
    i = pl.program_id(0)
    j = pl.program_id(1)
    halo = BF16_ROWS

    @pl.when(j == 0)
    def _():
        keep = (i % tiles_per_seq) != 0
        hext_ref[0:halo, :] = jnp.where(keep, halo_ref[...], jnp.zeros_like(halo_ref))
        hext_ref[halo:, :] = h_ref[...]
        acc_ref[...] = jnp.zeros_like(acc_ref)

    g = _dot(hext_ref[...], wg_ref[...])
    v = _dot(h_ref[...], wv_ref[...])
    cw = cw_ref[...]
    conv = cb_ref[...] + cw[FFN_CONV - 1:FFN_CONV, :] * g[halo:, :]
    for kk in range(FFN_CONV - 1):
        off = halo - (FFN_CONV - 1) + kk
        conv = conv + cw[kk:kk + 1, :] * g[off:off + tm, :]
    act = (jax.nn.silu(conv) * v).astype(BF16)
    acc_ref[...] += _dot(act, wd_ref[...])

    @pl.when(j == n_tiles - 1)
    def _():
        xn = x_ref[...] + acc_ref[...]
        xo_ref[...] = xn
        ho_ref[...] = _rms(xn, g_ref[...]).astype(ho_ref.dtype)


def ffn(h, w_up, conv_w, conv_b, w_down, layer, x, g, seq, h_dtype):
    t, d = h.shape
    n = w_up.shape[2] // 2
    tn = FFN_TILE_N
    tm = min(512, seq)
    n_tiles = n // tn
    halo = BF16_ROWS
    blocks = 2 * (_nbytes((tm, d), BF16) + 2 * _nbytes((d, tn), BF16) + _nbytes((tn, d), BF16)
                  + 2 * _nbytes((tm, d), F32) + _nbytes((tm, d), h_dtype))
    blocks += _nbytes((tm + halo, d), BF16) + _nbytes((tm, d), F32)
    blocks += 5 * _nbytes((tm, tn), F32) + 2 * _nbytes((tm, d), F32)
    kernel = functools.partial(_ffn_kernel, tm=tm, n_tiles=n_tiles, tiles_per_seq=seq // tm)
    return pl.pallas_call(
        kernel,
        grid=(t // tm, n_tiles),
        in_specs=[pl.BlockSpec((tm, d), lambda i, j: (i, 0)),
                  pl.BlockSpec((halo, d), lambda i, j: (jnp.maximum(i * (tm // halo) - 1, 0), 0)),
                  pl.BlockSpec((None, d, tn), lambda i, j: (layer, 0, j)),
                  pl.BlockSpec((None, d, tn), lambda i, j: (layer, 0, n_tiles + j)),
                  pl.BlockSpec((None, FFN_CONV, tn), lambda i, j: (layer, 0, j)),
                  pl.BlockSpec((None, 1, tn), lambda i, j: (layer, 0, j)),
                  pl.BlockSpec((None, tn, d), lambda i, j: (layer, j, 0)),
                  pl.BlockSpec((tm, d), lambda i, j: (i, 0)),
                  pl.BlockSpec((1, d), lambda i, j: (0, 0))],
        out_specs=[pl.BlockSpec((tm, d), lambda i, j: (i, 0)),
                   pl.BlockSpec((tm, d), lambda i, j: (i, 0))],
        out_shape=[jax.ShapeDtypeStruct((t, d), F32), jax.ShapeDtypeStruct((t, d), h_dtype)],
        scratch_shapes=[pltpu.VMEM((tm + halo, d), BF16), pltpu.VMEM((tm, d), F32)],
        compiler_params=_compiler_params(("parallel", "arbitrary"), blocks),
        name="ffn",
    )(h, h, w_up, w_up, conv_w, conv_b, w_down, x, g.reshape(1, d))


def _pad_last(a, n):
    return jnp.pad(a, [(0, 0)] * (a.ndim - 1) + [(0, n - a.shape[-1])])


def kernel(x, mem, mem_norm_g, norm_mix_g, w_in, attn_sinks, lru_conv_w, lru_conv_b, lru_wa, lru_ba,
           lru_wx, lru_bx, lru_lambda, w_branch, w_out, norm_xattn_g, xattn_wq, xattn_wkv, xattn_wo,
           norm_ffn_g, ffn_w_up, ffn_conv_w, ffn_conv_b, ffn_w_down, final_norm_g):
    batch, seq, d_model = x.shape
    depth = w_in.shape[0]
    mem_len = mem.shape[1]
    ffn_dim = ffn_w_down.shape[1]
    t = batch * seq
    assert seq % 512 == 0 or seq in (128, 256)
    assert d_model == 2048 and w_in.shape[2] == 13824

    col_ret, col_lru, col_gate = 0, 4 * RET_W, 4 * RET_W + 2 * LRU_WIDTH
    col_q = col_gate + 3 * d_model
    col_k = col_q + SWA_Q_W
    col_v = col_k + SWA_KV_W

    perm_heads = np.asarray(_q_head_order())
    q_cols = (perm_heads[:, None] * SWA_HEAD_DIM + np.arange(SWA_HEAD_DIM)[None, :]).reshape(-1)
    w_in_b = w_in.astype(BF16)
    w_head = jnp.concatenate([w_in[:, :, :SWA_Q_W][:, :, q_cols], w_in[:, :, SWA_Q_W:IN_TILE_N]],
                             axis=2).astype(BF16)
    w_branch_b = w_branch.astype(BF16)
    w_branch_a = w_branch[:, 0][:, q_cols, :].astype(BF16)
    w_out_b = w_out.astype(BF16)
    wq_b = xattn_wq.astype(BF16)
    wkv_b = xattn_wkv.astype(BF16)
    wo_b = xattn_wo.astype(BF16)
    lru_wa_b = lru_wa.astype(BF16)
    lru_wx_b = lru_wx.astype(BF16)
    ffn_pad = -(-ffn_dim // FFN_TILE_N) * FFN_TILE_N
    w_up = _pad_last(ffn_w_up.reshape(depth, d_model, 2, ffn_dim), ffn_pad).astype(BF16)
    w_up = w_up.reshape(depth, d_model, 2 * ffn_pad)
    w_down = jnp.pad(ffn_w_down, ((0, 0), (0, ffn_pad - ffn_dim), (0, 0))).astype(BF16)
    conv_w = _pad_last(ffn_conv_w, ffn_pad)
    conv_b = _pad_last(ffn_conv_b, ffn_pad)[:, None, :]
    cosf, sinf = _rope_tables(seq)

    xf = x.reshape(t, d_model)
    mem_n = rmsnorm(mem.reshape(batch * mem_len, d_model), mem_norm_g)
    h = rmsnorm(xf, norm_mix_g[0])
    for l in range(depth):
        proj = in_proj(h, w_head, w_in_b, l)
        o_a = swa_attention(proj, attn_sinks[l], seq, col_q, col_k, col_v)
        o_c = rg_lru_block(proj, lru_conv_w[l], lru_conv_b[l], lru_wa_b, lru_ba[l], lru_wx_b,
                           lru_bx[l], lru_lambda[l], l, batch, seq, col_lru)
        merged = retention_merge(proj, cosf, sinf, o_a, o_c, w_branch_a, w_branch_b, l, seq, col_ret,
                                 col_gate, d_model)
        xf, h = proj_res_norm(merged, w_out_b, l, xf, norm_xattn_g[l], "mixer_out")

        kv = matmul(mem_n, wkv_b, l, batch * mem_len, 1024, "xattn_kv")
        o_x = cross_attention(h, wq_b, l, kv, seq, mem_len)
        xf, h = proj_res_norm(o_x, wo_b, l, xf, norm_ffn_g[l], "xattn_out")

        last = l == depth - 1
        g_next = final_norm_g if last else norm_mix_g[l + 1]
        xf, h = ffn(h, w_up, conv_w, conv_b, w_down, l, xf, g_next, seq,
                    F32 if last else BF16)
    return h.reshape(batch, seq, d_model)
```

```python
import functools

import numpy as np
import jax
import jax.numpy as jnp
from jax import lax
from jax.experimental import pallas as pl
from jax.experimental.pallas import tpu as pltpu

NORM_EPS = 1e-6

LANES = 128
SUBLANES = 8
BF16_ROWS = 16
VMEM_BUDGET_BYTES = 58 * 1024 * 1024

SWA_Q_HEADS = 16
SWA_KV_HEADS = 4
SWA_HEAD_DIM = 64
SWA_BLOCK = 128
SWA_Q_W = SWA_Q_HEADS * SWA_HEAD_DIM
SWA_KV_W = SWA_KV_HEADS * SWA_HEAD_DIM

RET_HEADS = 8
RET_HEAD_DIM = 128
RET_CHUNK = 128
RET_W = RET_HEADS * RET_HEAD_DIM
RET_ROPE_BASE = 10000.0

LRU_WIDTH = 1024
LRU_BLOCKS = 8
LRU_BLOCK_DIM = LRU_WIDTH // LRU_BLOCKS
LRU_CONV = 4
LRU_C = 8.0

XATTN_HEADS = 4

FFN_CONV = 3
FFN_TILE_N = 512
MERGE_CHUNK = 1024
IN_TILE_N = SWA_Q_W + 2 * SWA_KV_W

LOG2E = 1.4426950408889634

BF16 = jnp.bfloat16
F32 = jnp.float32


def _compiler_params(semantics, vmem_bytes):
    return pltpu.CompilerParams(
        dimension_semantics=semantics,
        vmem_limit_bytes=int(min(max(vmem_bytes, 16 * 1024 * 1024), VMEM_BUDGET_BYTES)))


def _nbytes(shape, dtype):
    return int(np.prod(shape)) * jnp.dtype(dtype).itemsize


def _dot(a, b):
    return jnp.dot(a, b, preferred_element_type=F32)


def _dot_nt(a, b):
    return lax.dot_general(a, b, (((1,), (1,)), ((), ())), preferred_element_type=F32)


def _rms(x, g):
    ms = jnp.mean(x * x, axis=-1, keepdims=True)
    return x * lax.rsqrt(ms + NORM_EPS) * g


def _rmsnorm_kernel(x_ref, g_ref, o_ref):
    o_ref[...] = _rms(x_ref[...], g_ref[...]).astype(o_ref.dtype)


def rmsnorm(x, g, out_dtype=BF16):
    t, d = x.shape
    tm = min(512, t)
    blocks = 2 * (_nbytes((tm, d), F32) + _nbytes((tm, d), out_dtype)) + 2 * _nbytes((tm, d), F32)
    return pl.pallas_call(
        _rmsnorm_kernel,
        grid=(t // tm,),
        in_specs=[pl.BlockSpec((tm, d), lambda i: (i, 0)),
                  pl.BlockSpec((1, d), lambda i: (0, 0))],
        out_specs=pl.BlockSpec((tm, d), lambda i: (i, 0)),
        out_shape=jax.ShapeDtypeStruct((t, d), out_dtype),
        compiler_params=_compiler_params(("parallel",), blocks),
        name="rmsnorm",
    )(x, g.reshape(1, d))


def _matmul_kernel(a_ref, w_ref, o_ref):
    o_ref[...] = _dot(a_ref[...], w_ref[...]).astype(o_ref.dtype)


def matmul(a, w, layer, tm, tn, name):
    m, k = a.shape
    n = w.shape[2]
    tm = min(tm, m)
    blocks = (2 * (_nbytes((tm, k), BF16) + _nbytes((k, tn), BF16) + _nbytes((tm, tn), BF16))
              + 2 * _nbytes((tm, tn), F32))
    return pl.pallas_call(
        _matmul_kernel,
        grid=(m // tm, n // tn),
        in_specs=[pl.BlockSpec((tm, k), lambda i, j: (i, 0)),
                  pl.BlockSpec((None, k, tn), lambda i, j: (layer, 0, j))],
        out_specs=pl.BlockSpec((tm, tn), lambda i, j: (i, j)),
        out_shape=jax.ShapeDtypeStruct((m, n), BF16),
        compiler_params=_compiler_params(("parallel", "arbitrary"), blocks),
        name=name,
    )(a, w)


def _in_proj_kernel(h_ref, wh_ref, w_ref, o_ref):
    j = pl.program_id(1)

    @pl.when(j == 0)
    def _():
        o_ref[...] = _dot(h_ref[...], wh_ref[...]).astype(o_ref.dtype)

    @pl.when(j > 0)
    def _():
        o_ref[...] = _dot(h_ref[...], w_ref[...]).astype(o_ref.dtype)


def in_proj(h, w_head, w_in, layer):
    m, k = h.shape
    n = w_in.shape[2]
    tn = IN_TILE_N
    assert w_head.shape[2] == tn
    tm = min(1024, m)
    nt = n // tn
    blocks = (2 * (_nbytes((tm, k), BF16) + 2 * _nbytes((k, tn), BF16) + _nbytes((tm, tn), BF16))
              + 2 * _nbytes((tm, tn), F32))
    return pl.pallas_call(
        _in_proj_kernel,
        grid=(m // tm, nt),
        in_specs=[pl.BlockSpec((tm, k), lambda i, j: (i, 0)),
                  pl.BlockSpec((None, k, tn), lambda i, j: (layer, 0, 0)),
                  pl.BlockSpec((None, k, tn), lambda i, j: (layer, 0, jnp.maximum(j, 1)))],
        out_specs=pl.BlockSpec((tm, tn), lambda i, j: (i, (j + nt - 1) % nt)),
        out_shape=jax.ShapeDtypeStruct((m, n), BF16),
        compiler_params=_compiler_params(("parallel", "arbitrary"), blocks),
        name="in_proj",
    )(h, w_head, w_in)


def _q_head_order():
    order = []
    g = SWA_Q_HEADS // SWA_KV_HEADS
    for kv_pair in range(SWA_KV_HEADS // 2):
        for i in range(g):
            order += [(2 * kv_pair) * g + i, (2 * kv_pair + 1) * g + i]
    return order


def _swa_kernel(sinks_ref, q_ref, kc_ref, vc_ref, kp_ref, vp_ref, o_ref, *, tq, tiles_per_seq):
    i = pl.program_id(0)
    starts_seq = (i % tiles_per_seq) == 0
    blk = SWA_BLOCK
    kwin = jnp.concatenate([kp_ref[...], kc_ref[...]], axis=0)
    vwin = jnp.concatenate([vp_ref[...], vc_ref[...]], axis=0)
    lane = lax.broadcasted_iota(jnp.int32, (blk, LANES), 1)
    lo = lane < SWA_HEAD_DIM
    row = lax.broadcasted_iota(jnp.int32, (2 * blk, 2 * blk), 0)
    col = lax.broadcasted_iota(jnp.int32, (2 * blk, 2 * blk), 1)
    qi = jnp.bitwise_and(row, blk - 1)
    in_window = jnp.logical_and(col > qi, col <= qi + blk)
    in_window_first = jnp.logical_and(in_window, jnp.logical_or(col >= blk, jnp.logical_not(starts_seq)))
    top = lax.broadcasted_iota(jnp.int32, (2 * blk, 1), 0) < blk
    scale = (SWA_HEAD_DIM ** -0.5) * LOG2E
    order = _q_head_order()
    for c in range(tq // blk):
        allowed = in_window_first if c == 0 else in_window
        for p in range(SWA_Q_W // LANES):
            kv_blk = p // (SWA_Q_W // LANES // (SWA_KV_W // LANES))
            kb = kwin[c * blk:(c + 2) * blk, kv_blk * LANES:(kv_blk + 1) * LANES]
            vb = vwin[c * blk:(c + 2) * blk, kv_blk * LANES:(kv_blk + 1) * LANES]
            qb = q_ref[c * blk:(c + 1) * blk, p * LANES:(p + 1) * LANES]
            zero = jnp.zeros_like(qb)
            qs = jnp.concatenate([jnp.where(lo, qb, zero), jnp.where(lo, zero, qb)], axis=0)
            s = _dot_nt(qs, kb) * scale
            s = jnp.where(allowed, s, -jnp.inf)
            sink = jnp.where(top, sinks_ref[order[2 * p]], sinks_ref[order[2 * p + 1]]) * LOG2E
            m = jnp.maximum(jnp.max(s, axis=-1, keepdims=True), sink)
            e = jnp.exp2(s - m)
            denom = jnp.sum(e, axis=-1, keepdims=True) + jnp.exp2(sink - m)
            probs = (e * (1.0 / denom)).astype(BF16)
            o = _dot(probs, vb)
            ob = jnp.where(lo, o[:blk], o[blk:])
            o_ref[c * blk:(c + 1) * blk, p * LANES:(p + 1) * LANES] = ob.astype(o_ref.dtype)


def swa_attention(proj, sinks, seq, col_q, col_k, col_v):
    t = proj.shape[0]
    tq = min(2048, seq)
    nsub = tq // SWA_BLOCK
    qb, kb, vb = col_q // SWA_Q_W, col_k // SWA_KV_W, col_v // SWA_KV_W
    blocks = 2 * (2 * _nbytes((tq, SWA_Q_W), BF16) + 2 * _nbytes((tq + SWA_BLOCK, SWA_KV_W), BF16))
    blocks += 16 * _nbytes((2 * SWA_BLOCK, 2 * SWA_BLOCK), F32)
    kernel = functools.partial(_swa_kernel, tq=tq, tiles_per_seq=seq // tq)
    return pl.pallas_call(
        kernel,
        grid=(t // tq,),
        in_specs=[
            pl.BlockSpec(memory_space=pltpu.SMEM),
            pl.BlockSpec((tq, SWA_Q_W), lambda i: (i, qb)),
            pl.BlockSpec((tq, SWA_KV_W), lambda i: (i, kb)),
            pl.BlockSpec((tq, SWA_KV_W), lambda i: (i, vb)),
            pl.BlockSpec((SWA_BLOCK, SWA_KV_W), lambda i: (jnp.maximum(i * nsub - 1, 0), kb)),
            pl.BlockSpec((SWA_BLOCK, SWA_KV_W), lambda i: (jnp.maximum(i * nsub - 1, 0), vb)),
        ],
        out_specs=pl.BlockSpec((tq, SWA_Q_W), lambda i: (i, 0)),
        out_shape=jax.ShapeDtypeStruct((t, SWA_Q_W), BF16),
        compiler_params=_compiler_params(("parallel",), blocks),
        name="swa_attention",
    )(sinks, proj, proj, proj, proj, proj)


_RET_LOG_GAMMA = [float(v) for v in
                  np.log1p(-np.exp2(-5.0 - np.arange(RET_HEADS, dtype=np.float32))).astype(np.float32)]


def _retention_merge_kernel(q_ref, k_ref, v_ref, g_ref, cos_ref, sin_ref, oa_ref, oc_ref, wa_ref,
                            wb_ref, wc_ref, ga_ref, gb_ref, gc_ref, m_ref, ob_scr, state_ref,
                            decay_ref, *, chunks, tiles_per_seq, n_tiles):
    c = RET_CHUNK
    d = RET_HEAD_DIM
    i = pl.program_id(0)
    slot = i % 2
    tile = jnp.minimum(i, n_tiles - 1)

    @pl.when(i == 0)
    def _():
        ob_scr[...] = jnp.zeros_like(ob_scr)
        qpos = lax.broadcasted_iota(jnp.int32, (c, c), 0)
        kpos = lax.broadcasted_iota(jnp.int32, (c, c), 1)
        diff = (qpos - kpos).astype(F32)
        for h in range(RET_HEADS):
            decay_ref[h] = jnp.where(diff >= 0, jnp.exp(jnp.maximum(diff, 0.0) * _RET_LOG_GAMMA[h]), 0.0)

    @pl.when(tile % tiles_per_seq == 0)
    def _():
        state_ref[...] = jnp.zeros_like(state_ref)

    ob_prev = ob_scr.at[1 - slot]
    ob_cur = ob_scr.at[slot]
    m_chunk = MERGE_CHUNK
    heads_per_chunk = RET_HEADS * m_chunk // m_ref.shape[1]
    idx = lax.broadcasted_iota(jnp.int32, (c, 1), 0).astype(F32)
    for h in range(RET_HEADS):
        if h % heads_per_chunk == 0:
            mc = slice(h // heads_per_chunk * m_chunk, (h // heads_per_chunk + 1) * m_chunk)
            def gated(g_ref, o, w_ref):
                return jax.nn.sigmoid(g_ref[:, mc].astype(F32)) * _dot(o, w_ref[:, mc])

            merged = gated(ga_ref, oa_ref[...], wa_ref)
            merged = merged + gated(gb_ref, ob_prev[...], wb_ref)
            merged = merged + gated(gc_ref, oc_ref[...], wc_ref)
            m_ref[:, mc] = merged.astype(m_ref.dtype)

        lg = _RET_LOG_GAMMA[h]
        sl = slice(h * d, (h + 1) * d)
        q_decay = jnp.exp((idx + 1.0) * lg)
        k_decay = jnp.exp((c - 1.0 - idx) * lg)
        chunk_decay = float(np.exp(np.float32(c) * np.float32(lg)))
        state = state_ref[h]
        for n in range(chunks):
            rows = slice(n * c, (n + 1) * c)
            cosf = cos_ref[rows, :]
            sinf = sin_ref[rows, :]
            q = q_ref[rows, sl].astype(F32)
            k = k_ref[rows, sl].astype(F32)
            v = v_ref[rows, sl]
            qr = q * cosf + pltpu.roll(q, d // 2, axis=1) * sinf
            kr = (k * cosf + pltpu.roll(k, d // 2, axis=1) * sinf) * (d ** -0.5)
            scores = _dot_nt(qr.astype(BF16), kr.astype(BF16)) * decay_ref[h]
            intra = _dot(scores.astype(BF16), v)
            inter = _dot((qr * q_decay).astype(BF16), state.astype(BF16))
            kd_t = (kr * k_decay).T.astype(BF16)
            state = state * chunk_decay + _dot(kd_t, v)
            y = intra + inter
            mu = jnp.mean(y, axis=-1, keepdims=True)
            yc = y - mu
            yn = yc * lax.rsqrt(jnp.mean(yc * yc, axis=-1, keepdims=True) + NORM_EPS)
            gate = g_ref[rows, sl].astype(F32)
            ob_cur[rows, sl] = (jax.nn.silu(gate) * yn).astype(ob_scr.dtype)
        state_ref[h] = state


def _rope_tables(seq):
    half = RET_HEAD_DIM // 2
    theta = RET_ROPE_BASE ** (-jnp.linspace(0.0, 1.0, half, dtype=F32))
    ang = jnp.arange(seq, dtype=F32)[:, None] * theta[None, :]
    cos = jnp.cos(ang)
    sin = jnp.sin(ang)
    return jnp.concatenate([cos, cos], axis=-1), jnp.concatenate([-sin, sin], axis=-1)


def retention_merge(proj, cosf, sinf, o_a, o_c, w_branch_a, w_branch, layer, seq, col_q, col_gate,
                    d_model):
    t = proj.shape[0]
    c = RET_CHUNK
    tr = min(4 * c, seq)
    nt = seq // tr
    n_tiles = t // tr
    b0 = col_q // RET_W
    g0 = col_gate // d_model
    blocks = 2 * (6 * _nbytes((tr, RET_W), BF16) + 4 * _nbytes((tr, d_model), BF16))
    blocks += 4 * _nbytes((tr, RET_HEAD_DIM), F32) + 3 * _nbytes((RET_W, d_model), BF16)
    blocks += 2 * _nbytes((tr, RET_W), BF16) + 4 * _nbytes((tr, MERGE_CHUNK), F32)
    blocks += 2 * _nbytes((RET_HEADS, RET_HEAD_DIM, RET_HEAD_DIM), F32) + 96 * _nbytes((c, c), F32)
    kernel = functools.partial(_retention_merge_kernel, chunks=tr // c, tiles_per_seq=nt,
                               n_tiles=n_tiles)

    def cur(i):
        return jnp.minimum(i, n_tiles - 1)

    def prev(i):
        return jnp.maximum(i - 1, 0)

    def col(j):
        return pl.BlockSpec((tr, RET_W), lambda i: (cur(i), b0 + j))

    def gate(j):
        return pl.BlockSpec((tr, d_model), lambda i: (prev(i), g0 + j))

    def weight(j):
        return pl.BlockSpec((None, None, RET_W, d_model), lambda i: (layer, j, 0, 0),
                            pipeline_mode=pl.Buffered(1))

    o_prev = pl.BlockSpec((tr, RET_W), lambda i: (prev(i), 0))
    return pl.pallas_call(
        kernel,
        grid=(n_tiles + 1,),
        in_specs=[col(0), col(1), col(2), col(3),
                  pl.BlockSpec((tr, RET_HEAD_DIM), lambda i: (cur(i) % nt, 0)),
                  pl.BlockSpec((tr, RET_HEAD_DIM), lambda i: (cur(i) % nt, 0)),
                  o_prev, o_prev,
                  pl.BlockSpec((None, RET_W, d_model), lambda i: (layer, 0, 0),
                               pipeline_mode=pl.Buffered(1)),
                  weight(1), weight(2), gate(0), gate(1), gate(2)],
        out_specs=pl.BlockSpec((tr, d_model), lambda i: (prev(i), 0)),
        out_shape=jax.ShapeDtypeStruct((t, d_model), BF16),
        scratch_shapes=[pltpu.VMEM((2, tr, RET_W), BF16),
                        pltpu.VMEM((RET_HEADS, RET_HEAD_DIM, RET_HEAD_DIM), F32),
                        pltpu.VMEM((RET_HEADS, c, c), F32)],
        compiler_params=_compiler_params(("arbitrary",), blocks),
        name="retention_merge",
    )(proj, proj, proj, proj, cosf, sinf, o_a, o_c, w_branch_a, w_branch, w_branch, proj, proj, proj)


def _lru_kernel(x_ref, y_ref, cw_ref, cb_ref, wa_ref, ba_ref, wx_ref, bx_ref, lam_ref, o_ref,
                xprev_ref, carry_ref, *, tt):
    w = LRU_WIDTH

    @pl.when(pl.program_id(1) == 0)
    def _():
        xprev_ref[...] = jnp.zeros_like(xprev_ref)
        carry_ref[...] = jnp.zeros_like(carry_ref)

    x = x_ref[...].astype(F32)
    xe = jnp.concatenate([xprev_ref[...], x], axis=0)
    xprev_ref[...] = x[tt - SUBLANES:, :]
    cw = cw_ref[...]
    xc = cb_ref[...] + cw[3:4, :] * x
    for kk in range(LRU_CONV - 1):
        off = SUBLANES - (LRU_CONV - 1) + kk
        xc = xc + cw[kk:kk + 1, :] * xe[off:off + tt, :]

    xcb = xc.astype(BF16)
    gr = []
    gi = []
    for blk in range(LRU_BLOCKS):
        sl = slice(blk * LRU_BLOCK_DIM, (blk + 1) * LRU_BLOCK_DIM)
        gr.append(_dot(xcb[:, sl], wa_ref[blk]))
        gi.append(_dot(xcb[:, sl], wx_ref[blk]))
    gate_r = jax.nn.sigmoid(jnp.concatenate(gr, axis=1) + ba_ref[...])
    gate_i = jax.nn.sigmoid(jnp.concatenate(gi, axis=1) + bx_ref[...])
    z = -lam_ref[...]
    softplus = jnp.maximum(z, 0.0) + jnp.log1p(jnp.exp(-jnp.abs(z)))
    log_a = (-LRU_C) * gate_r * softplus
    a = jnp.exp(log_a)
    u = jnp.sqrt(1.0 - a * a) * (gate_i * xc)

    nv = tt // SUBLANES
    a3 = a.reshape(nv, SUBLANES, w)
    b3 = u.reshape(nv, SUBLANES, w)
    rix = lax.broadcasted_iota(jnp.int32, (nv, SUBLANES, w), 1)
    s = 1
    while s < SUBLANES:
        keep = rix >= s
        a_sh = jnp.where(keep, pltpu.roll(a3, s, axis=1), 1.0)
        b_sh = jnp.where(keep, pltpu.roll(b3, s, axis=1), 0.0)
        b3 = a3 * b_sh + b3
        a3 = a3 * a_sh
        s *= 2
    carry = carry_ref[...]
    hs = []
    for i in range(nv):
        h_i = a3[i] * carry + b3[i]
        carry = h_i[SUBLANES - 1:SUBLANES, :]
        hs.append(h_i)
    carry_ref[...] = carry
    h = jnp.concatenate(hs, axis=0)
    o_ref[...] = (h * jax.nn.gelu(y_ref[...].astype(F32))).astype(o_ref.dtype)


def rg_lru_block(proj, conv_w, conv_b, wa, ba, wx, bx, lam, layer, batch, seq, col_x):
    t = proj.shape[0]
    w = LRU_WIDTH
    tt = min(1024, seq)
    nt = seq // tt
    b0 = col_x // w
    blocks = 2 * 3 * _nbytes((tt, w), BF16) + 24 * _nbytes((tt, w), F32)
    blocks += 4 * _nbytes((LRU_BLOCKS, LRU_BLOCK_DIM, LRU_BLOCK_DIM), BF16)
    kernel = functools.partial(_lru_kernel, tt=tt)
    vec = pl.BlockSpec((1, w), lambda b, n: (0, 0))
    blockdiag = pl.BlockSpec((None, LRU_BLOCKS, LRU_BLOCK_DIM, LRU_BLOCK_DIM),
                             lambda b, n: (layer, 0, 0, 0))
    return pl.pallas_call(
        kernel,
        grid=(batch, nt),
        in_specs=[pl.BlockSpec((tt, w), lambda b, n: (b * nt + n, b0)),
                  pl.BlockSpec((tt, w), lambda b, n: (b * nt + n, b0 + 1)),
                  pl.BlockSpec((LRU_CONV, w), lambda b, n: (0, 0)),
                  vec, blockdiag, vec, blockdiag, vec, vec],
        out_specs=pl.BlockSpec((tt, w), lambda b, n: (b * nt + n, 0)),
        out_shape=jax.ShapeDtypeStruct((t, w), BF16),
        scratch_shapes=[pltpu.VMEM((SUBLANES, w), F32), pltpu.VMEM((1, w), F32)],
        compiler_params=_compiler_params(("arbitrary", "arbitrary"), blocks),
        name="rg_lru",
    )(proj, proj, conv_w, conv_b.reshape(1, w), wa, ba.reshape(1, w), wx, bx.reshape(1, w),
      lam.reshape(1, w))


def _proj_res_norm_kernel(a_ref, w_ref, x_ref, g_ref, xo_ref, ho_ref):
    xn = x_ref[...] + _dot(a_ref[...], w_ref[...])
    xo_ref[...] = xn
    ho_ref[...] = _rms(xn, g_ref[...]).astype(ho_ref.dtype)


def proj_res_norm(a, w, layer, x, g, name):
    t, kdim = a.shape
    d = w.shape[2]
    tm = min(512, t)
    blocks = 2 * (_nbytes((tm, kdim), BF16) + _nbytes((kdim, d), BF16) + 2 * _nbytes((tm, d), F32)
                  + _nbytes((tm, d), BF16)) + 3 * _nbytes((tm, d), F32)
    return pl.pallas_call(
        _proj_res_norm_kernel,
        grid=(t // tm,),
        in_specs=[pl.BlockSpec((tm, kdim), lambda i: (i, 0)),
                  pl.BlockSpec((None, kdim, d), lambda i: (layer, 0, 0)),
                  pl.BlockSpec((tm, d), lambda i: (i, 0)),
                  pl.BlockSpec((1, d), lambda i: (0, 0))],
        out_specs=[pl.BlockSpec((tm, d), lambda i: (i, 0)),
                   pl.BlockSpec((tm, d), lambda i: (i, 0))],
        out_shape=[jax.ShapeDtypeStruct((t, d), F32), jax.ShapeDtypeStruct((t, d), BF16)],
        compiler_params=_compiler_params(("parallel",), blocks),
        name=name,
    )(a, w, x, g.reshape(1, d))


def _xattn_kernel(h_ref, wq_ref, k_ref, v_ref, o_ref, *, head_dim):
    q = _dot(h_ref[...], wq_ref[...]).astype(BF16)
    scale = (head_dim ** -0.5) * LOG2E
    for hd in range(XATTN_HEADS):
        sl = slice(hd * head_dim, (hd + 1) * head_dim)
        s = _dot_nt(q[:, sl], k_ref[:, sl]) * scale
        m = jnp.max(s, axis=-1, keepdims=True)
        e = jnp.exp2(s - m)
        p = (e * (1.0 / jnp.sum(e, axis=-1, keepdims=True))).astype(BF16)
        o_ref[:, sl] = _dot(p, v_ref[:, sl]).astype(o_ref.dtype)


def cross_attention(h, wq, layer, kv, seq, mem_len):
    t, d = h.shape
    tm = min(1024, seq)
    tiles_per_seq = seq // tm
    blocks = 2 * (2 * _nbytes((tm, d), BF16) + 2 * _nbytes((mem_len, d), BF16)) + _nbytes((d, d), BF16)
    blocks += 3 * _nbytes((tm, d), F32)
    kernel = functools.partial(_xattn_kernel, head_dim=d // XATTN_HEADS)
    return pl.pallas_call(
        kernel,
        grid=(t // tm,),
        in_specs=[pl.BlockSpec((tm, d), lambda i: (i, 0)),
                  pl.BlockSpec((None, d, d), lambda i: (layer, 0, 0), pipeline_mode=pl.Buffered(1)),
                  pl.BlockSpec((mem_len, d), lambda i: (i // tiles_per_seq, 0)),
                  pl.BlockSpec((mem_len, d), lambda i: (i // tiles_per_seq, 1))],
        out_specs=pl.BlockSpec((tm, d), lambda i: (i, 0)),
        out_shape=jax.ShapeDtypeStruct((t, d), BF16),
        compiler_params=_compiler_params(("parallel",), blocks),
        name="cross_attention",
    )(h, wq, kv, kv)


def _ffn_kernel(h_ref, halo_ref, wg_ref, wv_ref, cw_ref, cb_ref, wd_ref, x_ref, g_ref, xo_ref, ho_ref,
                hext_ref, acc_ref, *, tm, n_tiles, tiles_per_seq):
    i = pl.program_id(0)
    j = pl.program_id(1)
    halo = BF16_ROWS

    @pl.when(j == 0)
    def _():
        keep = (i % tiles_per_seq) != 0
        hext_ref[0:halo, :] = jnp.where(keep, halo_ref[...], jnp.zeros_like(halo_ref))
        hext_ref[halo:, :] = h_ref[...]
        acc_ref[...] = jnp.zeros_like(acc_ref)

    g = _dot(hext_ref[...], wg_ref[...])
    v = _dot(h_ref[...], wv_ref[...])
    cw = cw_ref[...]
    conv = cb_ref[...] + cw[FFN_CONV - 1:FFN_CONV, :] * g[halo:, :]
    for kk in range(FFN_CONV - 1):
        off = halo - (FFN_CONV - 1) + kk
        conv = conv + cw[kk:kk + 1, :] * g[off:off + tm, :]
    act = (jax.nn.silu(conv) * v).astype(BF16)
    acc_ref[...] += _dot(act, wd_ref[...])

    @pl.when(j == n_tiles - 1)
    def _():
        xn = x_ref[...] + acc_ref[...]
        xo_ref[...] = xn
        ho_ref[...] = _rms(xn, g_ref[...]).astype(ho_ref.dtype)


def ffn(h, w_gate, w_val, conv_w, conv_b, w_down, layer, x, g, seq, h_dtype):
    t, d = h.shape
    n = w_gate.shape[2]
    tn = FFN_TILE_N
    tm = min(512, seq)
    n_tiles = n // tn
    halo = BF16_ROWS
    blocks = 2 * (_nbytes((tm, d), BF16) + 2 * _nbytes((d, tn), BF16) + _nbytes((tn, d), BF16)
                  + 2 * _nbytes((tm, d), F32) + _nbytes((tm, d), h_dtype))
    blocks += _nbytes((tm + halo, d), BF16) + _nbytes((tm, d), F32)
    blocks += 5 * _nbytes((tm, tn), F32) + 2 * _nbytes((tm, d), F32)
    kernel = functools.partial(_ffn_kernel, tm=tm, n_tiles=n_tiles, tiles_per_seq=seq // tm)
    return pl.pallas_call(
        kernel,
        grid=(t // tm, n_tiles),
        in_specs=[pl.BlockSpec((tm, d), lambda i, j: (i, 0)),
                  pl.BlockSpec((halo, d), lambda i, j: (jnp.maximum(i * (tm // halo) - 1, 0), 0)),
                  pl.BlockSpec((None, d, tn), lambda i, j: (layer, 0, j)),
                  pl.BlockSpec((None, d, tn), lambda i, j: (layer, 0, j)),
                  pl.BlockSpec((None, FFN_CONV, tn), lambda i, j: (layer, 0, j)),
                  pl.BlockSpec((None, 1, tn), lambda i, j: (layer, 0, j)),
                  pl.BlockSpec((None, tn, d), lambda i, j: (layer, j, 0)),
                  pl.BlockSpec((tm, d), lambda i, j: (i, 0)),
                  pl.BlockSpec((1, d), lambda i, j: (0, 0))],
        out_specs=[pl.BlockSpec((tm, d), lambda i, j: (i, 0)),
                   pl.BlockSpec((tm, d), lambda i, j: (i, 0))],
        out_shape=[jax.ShapeDtypeStruct((t, d), F32), jax.ShapeDtypeStruct((t, d), h_dtype)],
        scratch_shapes=[pltpu.VMEM((tm + halo, d), BF16), pltpu.VMEM((tm, d), F32)],
        compiler_params=_compiler_params(("parallel", "arbitrary"), blocks),
        name="ffn",
    )(h, h, w_gate, w_val, conv_w, conv_b, w_down, x, g.reshape(1, d))


def _pad_last(a, n):
    return jnp.pad(a, [(0, 0)] * (a.ndim - 1) + [(0, n - a.shape[-1])])


def kernel(x, mem, mem_norm_g, norm_mix_g, w_in, attn_sinks, lru_conv_w, lru_conv_b, lru_wa, lru_ba,
           lru_wx, lru_bx, lru_lambda, w_branch, w_out, norm_xattn_g, xattn_wq, xattn_wkv, xattn_wo,
           norm_ffn_g, ffn_w_up, ffn_conv_w, ffn_conv_b, ffn_w_down, final_norm_g):
    batch, seq, d_model = x.shape
    depth = w_in.shape[0]
    mem_len = mem.shape[1]
    ffn_dim = ffn_w_down.shape[1]
    t = batch * seq
    assert seq % 512 == 0 or seq in (128, 256)
    assert d_model == 2048 and w_in.shape[2] == 13824

    col_ret, col_lru, col_gate = 0, 4 * RET_W, 4 * RET_W + 2 * LRU_WIDTH
    col_q = col_gate + 3 * d_model
    col_k = col_q + SWA_Q_W
    col_v = col_k + SWA_KV_W

    perm_heads = np.asarray(_q_head_order())
    q_cols = (perm_heads[:, None] * SWA_HEAD_DIM + np.arange(SWA_HEAD_DIM)[None, :]).reshape(-1)
    w_in_b = w_in.astype(BF16)
    w_head = jnp.concatenate([w_in[:, :, :SWA_Q_W][:, :, q_cols], w_in[:, :, SWA_Q_W:IN_TILE_N]],
                             axis=2).astype(BF16)
    w_branch_b = w_branch.astype(BF16)
    w_branch_a = w_branch[:, 0][:, q_cols, :].astype(BF16)
    w_out_b = w_out.astype(BF16)
    wq_b = xattn_wq.astype(BF16)
    wkv_b = xattn_wkv.astype(BF16)
    wo_b = xattn_wo.astype(BF16)
    lru_wa_b = lru_wa.astype(BF16)
    lru_wx_b = lru_wx.astype(BF16)
    ffn_pad = -(-ffn_dim // FFN_TILE_N) * FFN_TILE_N
    w_gate = _pad_last(ffn_w_up[:, :, :ffn_dim], ffn_pad).astype(BF16)
    w_val = _pad_last(ffn_w_up[:, :, ffn_dim:], ffn_pad).astype(BF16)
    w_down = jnp.pad(ffn_w_down, ((0, 0), (0, ffn_pad - ffn_dim), (0, 0))).astype(BF16)
    conv_w = _pad_last(ffn_conv_w, ffn_pad)
    conv_b = _pad_last(ffn_conv_b, ffn_pad)[:, None, :]
    cosf, sinf = _rope_tables(seq)

    xf = x.reshape(t, d_model)
    mem_n = rmsnorm(mem.reshape(batch * mem_len, d_model), mem_norm_g)
    h = rmsnorm(xf, norm_mix_g[0])
    for l in range(depth):
        proj = in_proj(h, w_head, w_in_b, l)
        o_a = swa_attention(proj, attn_sinks[l], seq, col_q, col_k, col_v)
        o_c = rg_lru_block(proj, lru_conv_w[l], lru_conv_b[l], lru_wa_b, lru_ba[l], lru_wx_b,
                           lru_bx[l], lru_lambda[l], l, batch, seq, col_lru)
        merged = retention_merge(proj, cosf, sinf, o_a, o_c, w_branch_a, w_branch_b, l, seq, col_ret,
                                 col_gate, d_model)
        xf, h = proj_res_norm(merged, w_out_b, l, xf, norm_xattn_g[l], "mixer_out")

        kv = matmul(mem_n, wkv_b, l, batch * mem_len, 1024, "xattn_kv")
        o_x = cross_attention(h, wq_b, l, kv, seq, mem_len)
        xf, h = proj_res_norm(o_x, wo_b, l, xf, norm_ffn_g[l], "xattn_out")

        last = l == depth - 1
        g_next = final_norm_g if last else norm_mix_g[l + 1]
        xf, h = ffn(h, w_gate, w_val, conv_w, conv_b, w_down, l, xf, g_next, seq,
                    F32 if last else BF16)
    return h.reshape(batch, seq, d_model)
```
